```python
import math
import jax, jax.numpy as jnp
from jax import lax
import numpy as np

D_MODEL = 1024
BATCH = 32
SEQ = 2048
DEPTH = 2
DEC_BATCH = 8
DEC_SEQ = 16
PAST_LEN = 1024

CHUNK = 64
N_META = 16
MIX_WIDTH = D_MODEL
ATTN_WIDTH = MIX_WIDTH // 2
POOL_WIDTH = MIX_WIDTH - ATTN_WIDTH
HEAD_DIM = 64
N_HEADS = ATTN_WIDTH // (2 * HEAD_DIM)
POOL_WINDOWS = (2, 4, 8, 16)
N_POOL_GROUPS = len(POOL_WINDOWS)
POOL_GROUP_WIDTH = POOL_WIDTH // N_POOL_GROUPS
POOL_STATE = max(POOL_WINDOWS) - 1
IN_COLS = 3 * ATTN_WIDTH + POOL_WIDTH
N_EXPERTS = 16
N_EXPERT_GROUPS = 4
EXPERTS_PER_GROUP = N_EXPERTS // N_EXPERT_GROUPS
TOP_K = 2
D_EXPERT = D_MODEL // 2
Q_BLOCK = 128
ALPHA = (2 * DEPTH) ** 0.25
BETA = (8 * DEPTH) ** -0.25
LN_EPS = 1e-5

kernel_name = 'hybrid_stream_diffattn_pool_grouped_moe'


def layer_norm(x, g, b):
    xf = x.astype(jnp.float32)
    mu = jnp.mean(xf, axis=-1, keepdims=True)
    var = jnp.mean(jnp.square(xf - mu), axis=-1, keepdims=True)
    y = (xf - mu) * lax.rsqrt(var + LN_EPS) * g.astype(jnp.float32) + b.astype(jnp.float32)
    return y.astype(x.dtype)


def alibi_slopes():
    return 2.0 ** (-8.0 * jnp.arange(1, N_HEADS + 1, dtype=jnp.float32) / N_HEADS)


def lambda_init(layer):
    return 0.8 - 0.6 * math.exp(-0.3 * layer)


def diff_lambda(lam_qk, lam_init):
    lq = lam_qk.astype(jnp.float32)
    return jnp.exp(jnp.sum(lq[0] * lq[1])) - jnp.exp(jnp.sum(lq[2] * lq[3])) + lam_init


def diff_attention(q, k, v, q_pos, k_pos, q_chunk, k_chunk, lam, subln_w, lam_init):
    q1, q2 = q[..., :HEAD_DIM].astype(jnp.float32), q[..., HEAD_DIM:].astype(jnp.float32)
    k1, k2 = k[..., :HEAD_DIM].astype(jnp.float32), k[..., HEAD_DIM:].astype(jnp.float32)
    dist = jnp.abs(q_pos[:, None] - k_pos[None, :]).astype(jnp.float32)
    bias = -alibi_slopes()[:, None, None] * dist[None]
    visible = k_chunk[None, :] <= q_chunk[:, None]
    scale = HEAD_DIM ** -0.5

    def softmax_map(qa, ka):
        s = jnp.einsum('bqhd,bkhd->bhqk', qa, ka) * scale + bias
        return jax.nn.softmax(jnp.where(visible, s, -jnp.inf), axis=-1)

    weights = softmax_map(q1, k1) - lam * softmax_map(q2, k2)
    o = jnp.einsum('bhqk,bkhe->bqhe', weights, v.astype(jnp.float32))
    o = o * lax.rsqrt(jnp.mean(jnp.square(o), axis=-1, keepdims=True) + LN_EPS)
    o = o * subln_w.astype(jnp.float32) * (1.0 - lam_init)
    return o.astype(q.dtype)


def prompt_attention(q, k, v, lam, subln_w, lam_init):
    B, L = q.shape[0], q.shape[1]
    pos = jnp.arange(L, dtype=jnp.int32)
    chunk = jnp.where(pos < N_META, -1, (pos - N_META) // CHUNK)
    meta_out = diff_attention(q[:, :N_META], k[:, :N_META], v[:, :N_META], pos[:N_META], pos[:N_META],
                              chunk[:N_META], chunk[:N_META], lam, subln_w, lam_init)
    n_blk = (L - N_META) // Q_BLOCK
    qb = q[:, N_META:].reshape(B, n_blk, Q_BLOCK, N_HEADS, 2 * HEAD_DIM).transpose(1, 0, 2, 3, 4)
    pb = pos[N_META:].reshape(n_blk, Q_BLOCK)
    cb = chunk[N_META:].reshape(n_blk, Q_BLOCK)

    def one_block(args):
        qi, pi, ci = args
        return diff_attention(qi, k, v, pi, pos, ci, chunk, lam, subln_w, lam_init)

    out = lax.map(one_block, (qb, pb, cb))
    real_out = out.transpose(1, 0, 2, 3, 4).reshape(B, L - N_META, N_HEADS, 2 * HEAD_DIM)
    return jnp.concatenate([meta_out, real_out], axis=1)


def pool_mixer(u_ext, pos0, n_out, w_pool_l, scale_l):
    B, T, _ = u_ext.shape
    uf = u_ext.astype(jnp.float32)
    cs = jnp.pad(jnp.cumsum(uf, axis=1), ((0, 0), (1, 0), (0, 0)))
    pos = pos0 + jnp.arange(T, dtype=jnp.int32)
    diffs = []
    for g, w in enumerate(POOL_WINDOWS):
        lo, hi = g * POOL_GROUP_WIDTH, (g + 1) * POOL_GROUP_WIDTH
        c = cs[..., lo:hi]
        lagged = jnp.pad(c, ((0, 0), (w, 0), (0, 0)))[:, 1:T + 1]
        window_sum = c[:, 1:] - lagged
        count = jnp.minimum(pos + 1, w).astype(jnp.float32)[:, None]
        diffs.append(window_sum / count - uf[..., lo:hi])
    d = jnp.stack(diffs, axis=2)[:, T - n_out:]
    y = jnp.einsum('btgc,gce->btge', d, w_pool_l.astype(jnp.float32))
    y = y * scale_l.astype(jnp.float32).reshape(N_POOL_GROUPS, POOL_GROUP_WIDTH)
    return y.reshape(B, n_out, POOL_WIDTH).astype(u_ext.dtype)


def split_projection(h, w_in_l):
    B, T, _ = h.shape
    proj = jnp.einsum('btd,de->bte', h, w_in_l)
    q = proj[..., :ATTN_WIDTH].reshape(B, T, N_HEADS, 2 * HEAD_DIM)
    k = proj[..., ATTN_WIDTH:2 * ATTN_WIDTH].reshape(B, T, N_HEADS, 2 * HEAD_DIM)
    v = proj[..., 2 * ATTN_WIDTH:3 * ATTN_WIDTH].reshape(B, T, N_HEADS, 2 * HEAD_DIM)
    u = proj[..., 3 * ATTN_WIDTH:]
    return q, k, v, u


def merge_and_residual(h, attn_out, pool_out, w_out_l, g, b):
    B, T, _ = h.shape
    mix = jnp.concatenate([attn_out.reshape(B, T, ATTN_WIDTH), pool_out], axis=-1)
    return layer_norm(ALPHA * h + jnp.einsum('bte,ed->btd', mix, w_out_l), g, b)


def moe_ffn(h, router_w, router_b, w_gate_l, w_up_l, w_down_l):
    B, T, D = h.shape
    xf = h.reshape(B * T, D)
    probs = jax.nn.softmax(jnp.einsum('nd,de->ne', xf.astype(jnp.float32), router_w.astype(jnp.float32)), axis=-1)
    sel = (probs + router_b.astype(jnp.float32)).reshape(-1, N_EXPERT_GROUPS, EXPERTS_PER_GROUP)
    group_score = jnp.sum(lax.top_k(sel, TOP_K)[0], axis=-1)
    g_idx = jnp.argmax(group_score, axis=-1)
    in_group = jnp.take_along_axis(sel, g_idx[:, None, None], axis=1)[:, 0]
    _, e_local = lax.top_k(in_group, TOP_K)
    e_idx = g_idx[:, None] * EXPERTS_PER_GROUP + e_local
    gate = jnp.take_along_axis(probs, e_idx, axis=1)
    gate = gate / jnp.sum(gate, axis=-1, keepdims=True)
    combine = jnp.einsum('nk,nke->ne', gate, jax.nn.one_hot(e_idx, N_EXPERTS, dtype=jnp.float32))
    y = jnp.zeros((B * T, D), jnp.float32)
    for e in range(N_EXPERTS):
        hid = jax.nn.silu(xf @ w_gate_l[e]) * (xf @ w_up_l[e])
        y = y + combine[:, e:e + 1] * (hid @ w_down_l[e]).astype(jnp.float32)
    return y.reshape(B, T, D).astype(h.dtype)


def setup_inputs(seed: int = 0) -> dict:
    key = jax.random.key(seed)
    ks = jax.random.split(key, 24)
    f32 = jnp.float32

    def nrm(k, shape, scale):
        return jax.random.normal(k, shape, f32) * scale

    v_col_scale = jnp.ones((IN_COLS,), f32).at[2 * ATTN_WIDTH:3 * ATTN_WIDTH].set(BETA)
    return {
        'x_prompt': nrm(ks[0], (BATCH, SEQ, D_MODEL), 1.0),
        'x_sample': nrm(ks[1], (DEC_BATCH, DEC_SEQ, D_MODEL), 1.0),
        'cache_k': nrm(ks[2], (DEPTH, DEC_BATCH, PAST_LEN, N_HEADS, 2 * HEAD_DIM), 1.0),
        'cache_v': nrm(ks[3], (DEPTH, DEC_BATCH, PAST_LEN, N_HEADS, 2 * HEAD_DIM), BETA),
        'state_pool': nrm(ks[4], (DEPTH, DEC_BATCH, POOL_STATE, POOL_WIDTH), 1.0),
        'meta_tokens': nrm(ks[5], (N_META, D_MODEL), 1.0),
        'ln_in_g': 1.0 + nrm(ks[6], (D_MODEL,), 0.05),
        'ln_in_b': nrm(ks[7], (D_MODEL,), 0.02),
        'w_in': nrm(ks[8], (DEPTH, D_MODEL, IN_COLS), D_MODEL ** -0.5) * v_col_scale,
        'lambda_qk': nrm(ks[9], (DEPTH, 4, HEAD_DIM), 0.1),
        'subln_w': 1.0 + nrm(ks[10], (DEPTH, 2 * HEAD_DIM), 0.05),
        'w_pool': nrm(ks[11], (DEPTH, N_POOL_GROUPS, POOL_GROUP_WIDTH, POOL_GROUP_WIDTH), POOL_GROUP_WIDTH ** -0.5),
        'pool_scale': 1.0 + nrm(ks[12], (DEPTH, POOL_WIDTH), 0.1),
        'w_out': nrm(ks[13], (DEPTH, MIX_WIDTH, D_MODEL), MIX_WIDTH ** -0.5 * BETA),
        'ln1_g': 1.0 + nrm(ks[14], (DEPTH, D_MODEL), 0.05),
        'ln1_b': nrm(ks[15], (DEPTH, D_MODEL), 0.02),
        'router_w': nrm(ks[16], (D_MODEL, N_EXPERTS), D_MODEL ** -0.5),
        'router_b': nrm(ks[17], (N_EXPERTS,), 0.01),
        'w_gate': nrm(ks[18], (DEPTH, N_EXPERTS, D_MODEL, D_EXPERT), D_MODEL ** -0.5),
        'w_up': nrm(ks[19], (DEPTH, N_EXPERTS, D_MODEL, D_EXPERT), D_MODEL ** -0.5),
        'w_down': nrm(ks[20], (DEPTH, N_EXPERTS, D_EXPERT, D_MODEL), D_EXPERT ** -0.5 * BETA),
        'ln2_g': 1.0 + nrm(ks[21], (DEPTH, D_MODEL), 0.05),
        'ln2_b': nrm(ks[22], (DEPTH, D_MODEL), 0.02),
    }


def reference(x_prompt, x_sample, cache_k, cache_v, state_pool, meta_tokens, ln_in_g, ln_in_b, w_in, lambda_qk,
              subln_w, w_pool, pool_scale, w_out, ln1_g, ln1_b, router_w, router_b, w_gate, w_up, w_down,
              ln2_g, ln2_b):
    B, S, D = x_prompt.shape
    L = N_META + S
    meta = jnp.broadcast_to(meta_tokens[None].astype(x_prompt.dtype), (B, N_META, D))
    h = layer_norm(jnp.concatenate([meta, x_prompt], axis=1), ln_in_g, ln_in_b)
    k_rows, v_rows, pool_rows = [], [], []
    for l in range(DEPTH):
        lam_init = lambda_init(l)
        lam = diff_lambda(lambda_qk[l], lam_init)
        q, k, v, u = split_projection(h, w_in[l])
        a = prompt_attention(q, k, v, lam, subln_w[l], lam_init)
        p = pool_mixer(u, 0, L, w_pool[l], pool_scale[l])
        h = merge_and_residual(h, a, p, w_out[l], ln1_g[l], ln1_b[l])
        h = layer_norm(ALPHA * h + moe_ffn(h, router_w, router_b, w_gate[l], w_up[l], w_down[l]), ln2_g[l], ln2_b[l])
        k_rows.append(k)
        v_rows.append(v)
        pool_rows.append(u[:, L - POOL_STATE:])
    y_prompt = h[:, N_META:]

    T = x_sample.shape[1]
    P = cache_k.shape[2]
    k_pos = jnp.arange(P + T, dtype=jnp.int32)
    q_pos = k_pos[P:]
    hs = layer_norm(x_sample, ln_in_g, ln_in_b)
    ks_rows, vs_rows, pools_rows = [], [], []
    for l in range(DEPTH):
        lam_init = lambda_init(l)
        lam = diff_lambda(lambda_qk[l], lam_init)
        q, k, v, u = split_projection(hs, w_in[l])
        k_all = jnp.concatenate([cache_k[l].astype(k.dtype), k], axis=1)
        v_all = jnp.concatenate([cache_v[l].astype(v.dtype), v], axis=1)
        a = diff_attention(q, k_all, v_all, q_pos, k_pos, q_pos // CHUNK, k_pos // CHUNK, lam, subln_w[l], lam_init)
        u_ext = jnp.concatenate([state_pool[l].astype(u.dtype), u], axis=1)
        p = pool_mixer(u_ext, P - POOL_STATE, T, w_pool[l], pool_scale[l])
        hs = merge_and_residual(hs, a, p, w_out[l], ln1_g[l], ln1_b[l])
        hs = layer_norm(ALPHA * hs + moe_ffn(hs, router_w, router_b, w_gate[l], w_up[l], w_down[l]), ln2_g[l], ln2_b[l])
        ks_rows.append(k)
        vs_rows.append(v)
        pools_rows.append(u_ext[:, T:])
    y_sample = hs

    new_k_prompt = jnp.stack(k_rows, axis=0)
    new_v_prompt = jnp.stack(v_rows, axis=0)
    new_pool_prompt = jnp.stack(pool_rows, axis=0)
    new_k_sample = jnp.stack(ks_rows, axis=0)
    new_v_sample = jnp.stack(vs_rows, axis=0)
    new_pool_sample = jnp.stack(pools_rows, axis=0)
    return (y_prompt, y_sample, new_k_prompt, new_v_prompt, new_pool_prompt, new_k_sample, new_v_sample, new_pool_sample)
```

```python
import functools
import math

import jax
import jax.numpy as jnp
from jax import lax
from jax.experimental import pallas as pl
from jax.experimental.pallas import tpu as pltpu

F32, BF16, I32 = jnp.float32, jnp.bfloat16, jnp.int32

CHUNK = 64
N_META = 16
HEAD_DIM = 64
HEAD_W = 2 * HEAD_DIM
POOL_WINDOWS = (2, 4, 8, 16)
POOL_HALO = 16
N_EXPERTS = 16
N_GROUPS = 4
EPG = N_EXPERTS // N_GROUPS
PAIRS = ((0, 1), (0, 2), (0, 3), (1, 2), (1, 3), (2, 3))
N_CLASSES = N_GROUPS * len(PAIRS)
LN_EPS = 1e-5
NEG = -1e30

TM = 512
TMOE = 256
ATT_T = 256
VMEM_LIMIT = 56 * 1024 * 1024


def _cparams(*sem):
    return pltpu.CompilerParams(dimension_semantics=sem, vmem_limit_bytes=VMEM_LIMIT)


def _ln(x, g, b):
    mu = jnp.mean(x, axis=-1, keepdims=True)
    xc = x - mu
    var = jnp.mean(xc * xc, axis=-1, keepdims=True)
    return xc * lax.rsqrt(var + LN_EPS) * g + b


def _dot(a, b):
    return jnp.dot(a, b, preferred_element_type=F32)


def _dot_nt(a, b):
    return lax.dot_general(a, b, (((1,), (1,)), ((), ())), preferred_element_type=F32)


def _k1_project(h, w_ref, h_ref, q_ref, u_ref, k_ref, v_ref, kx_ref, vx_ref, is_extras):
    aw = k_ref.shape[-1]
    h_ref[...] = h
    hb = h.astype(BF16)
    q_ref[...] = _dot(hb, w_ref[:, 0:aw]).astype(BF16)
    k = _dot(hb, w_ref[:, aw:2 * aw])
    v = _dot(hb, w_ref[:, 2 * aw:3 * aw])
    u_ref[...] = _dot(hb, w_ref[:, 3 * aw:])
    k_ref[...] = k
    v_ref[...] = v
    if is_extras:
        kx_ref[...] = k
        vx_ref[...] = v


def _k1_first_body(xe_ref, x_ref, g_ref, b_ref, w_ref, h_ref, q_ref, u_ref, k_ref, v_ref, kx_ref, vx_ref):
    outs = (w_ref, h_ref, q_ref, u_ref, k_ref, v_ref, kx_ref, vx_ref)

    @pl.when(pl.program_id(0) == 0)
    def _():
        _k1_project(_ln(xe_ref[...], g_ref[...], b_ref[...]), *outs, True)

    @pl.when(pl.program_id(0) > 0)
    def _():
        _k1_project(_ln(x_ref[...], g_ref[...], b_ref[...]), *outs, False)


def _k1_next_body(alpha, h1_ref, y_ref, g_ref, b_ref, w_ref, ko_ref, vo_ref,
                  h_ref, q_ref, u_ref, k_ref, v_ref, kx_ref, vx_ref):
    del ko_ref, vo_ref
    outs = (w_ref, h_ref, q_ref, u_ref, k_ref, v_ref, kx_ref, vx_ref)
    h = _ln(alpha * h1_ref[...] + y_ref[...].astype(F32), g_ref[...], b_ref[...])

    @pl.when(pl.program_id(0) == 0)
    def _():
        _k1_project(h, *outs, True)

    @pl.when(pl.program_id(0) > 0)
    def _():
        _k1_project(h, *outs, False)


def _k1_specs(layer, n_tiles, tiles_per_b, d, aw, pw):
    def frames_tile(i):
        return jnp.maximum(i - 1, 0)

    def kv_map(i):
        t = frames_tile(i)
        return (layer, t // tiles_per_b, pl.multiple_of(N_META + (t % tiles_per_b) * TM, 16), 0)

    kv_spec = pl.BlockSpec((None, None, pl.Element(TM), pl.Element(aw)), kv_map)
    row = lambda w: pl.BlockSpec((TM, w), lambda i: (i, 0))
    const = lambda r, w: pl.BlockSpec((r, w), lambda i: (0, 0))
    out_specs = [row(d), row(aw), row(pw), kv_spec, kv_spec, const(TM, aw), const(TM, aw)]
    return row, const, frames_tile, out_specs


def _k1_out_shapes(n, d, aw, pw, depth, b, l):
    return [
        jax.ShapeDtypeStruct((n, d), F32),
        jax.ShapeDtypeStruct((n, aw), BF16),
        jax.ShapeDtypeStruct((n, pw), F32),
        jax.ShapeDtypeStruct((depth, b, l, aw), F32),
        jax.ShapeDtypeStruct((depth, b, l, aw), F32),
        jax.ShapeDtypeStruct((TM, aw), F32),
        jax.ShapeDtypeStruct((TM, aw), F32),
    ]


def _k1_first(x_extra, x_frames, g, b, w, depth, bsz, seq):
    nf, d = x_frames.shape
    aw = w.shape[1] // 4
    pw = w.shape[1] - 3 * aw
    n = TM + nf
    n_tiles = n // TM
    tpb = seq // TM
    row, const, frames_tile, out_specs = _k1_specs(0, n_tiles, tpb, d, aw, pw)
    in_specs = [const(TM, d), pl.BlockSpec((TM, d), lambda i: (frames_tile(i), 0)),
                const(1, d), const(1, d), const(d, w.shape[1])]
    return pl.pallas_call(
        _k1_first_body, grid=(n_tiles,), in_specs=in_specs, out_specs=out_specs,
        out_shape=_k1_out_shapes(n, d, aw, pw, depth, bsz, N_META + seq),
        compiler_params=_cparams("arbitrary"), name="k1_first",
    )(x_extra, x_frames, g, b, w)


def _k1_next(layer, alpha, h1, y, g, b, w, k_out, v_out, seq):
    n, d = h1.shape
    depth, bsz, l, aw = k_out.shape
    pw = w.shape[1] - 3 * aw
    n_tiles = n // TM
    tpb = seq // TM
    row, const, _, out_specs = _k1_specs(layer, n_tiles, tpb, d, aw, pw)
    any_spec = pl.BlockSpec(memory_space=pl.ANY)
    in_specs = [row(d), row(d), const(1, d), const(1, d), const(d, w.shape[1]), any_spec, any_spec]
    return pl.pallas_call(
        functools.partial(_k1_next_body, alpha), grid=(n_tiles,), in_specs=in_specs, out_specs=out_specs,
        out_shape=_k1_out_shapes(n, d, aw, pw, depth, bsz, l),
        input_output_aliases={5: 3, 6: 4},
        compiler_params=_cparams("arbitrary"), name="k1_next",
    )(h1, y, g, b, w, k_out, v_out)


def _split3(c):
    hi = c.astype(BF16).astype(F32)
    r = c - hi
    mid = r.astype(BF16).astype(F32)
    return hi, mid, r - mid


def _attn_body(slope_ref, lam_ref, q_ref, k_ref, v_ref, sw_ref, o_ref, ka1, ka2, vb, qa1, qa2):
    t = ATT_T
    slope = slope_ref[pl.program_id(1)]
    lam = lam_ref[0]
    s_len = q_ref.shape[0]
    n_sub = s_len // t
    lane = lax.broadcasted_iota(I32, (t, HEAD_W), 1)
    rows = lax.broadcasted_iota(I32, (t, 1), 0)
    low = lane < HEAD_DIM

    def aug_cols(c):
        hi, mid, lo = _split3(c)
        return jnp.where(lane == HEAD_DIM, hi, jnp.where(lane == HEAD_DIM + 1, mid,
                         jnp.where(lane == HEAD_DIM + 2, lo, 0.0)))

    def put_keys(dst_row, kblk, vblk, c):
        augv = aug_cols(c)
        ka1[pl.ds(dst_row, t), :] = jnp.where(low, kblk, augv).astype(BF16)
        ka2[pl.ds(dst_row, t), :] = jnp.where(low, pltpu.roll(kblk, HEAD_DIM, 1), augv).astype(BF16)
        vb[pl.ds(dst_row, t), :] = vblk.astype(BF16)

    pad = t - N_META
    is_meta = rows >= pad
    kpad = jnp.concatenate([jnp.zeros((pad, HEAD_W), F32), k_ref[0:N_META, :]], axis=0)
    vpad = jnp.concatenate([jnp.zeros((pad, HEAD_W), F32), v_ref[0:N_META, :]], axis=0)
    c0 = jnp.where(is_meta, slope * (rows - pad).astype(F32), NEG)
    put_keys(0, kpad, vpad, c0)

    def prep(j, carry):
        r0 = pl.multiple_of(j * t, t)
        src = pl.multiple_of(N_META + j * t, 8)
        pos = (N_META + r0 + rows).astype(F32)
        put_keys(pl.multiple_of(t + r0, t), k_ref[pl.ds(src, t), :], v_ref[pl.ds(src, t), :], slope * pos)
        qf = q_ref[pl.ds(r0, t), :].astype(F32) * (HEAD_DIM ** -0.5)
        ones = jnp.where(lane < HEAD_DIM + 3, 1.0, 0.0)
        qa1[pl.ds(r0, t), :] = jnp.where(low, qf, ones).astype(BF16)
        qa2[pl.ds(r0, t), :] = jnp.where(low, pltpu.roll(qf, HEAD_DIM, 1), ones).astype(BF16)
        return carry

    lax.fori_loop(0, n_sub, prep, 0)

    rr = lax.broadcasted_iota(I32, (t, t), 0)
    cc = lax.broadcasted_iota(I32, (t, t), 1)
    diag_add = jnp.where((cc // CHUNK) <= (rr // CHUNK),
                         (-2.0 * slope) * jnp.maximum(cc - rr, 0).astype(F32), NEG)

    def sub_block(sb, carry):
        q0 = pl.multiple_of(sb * t, t)
        res = []
        for qa, ka in ((qa1, ka1), (qa2, ka2)):
            qblk = qa[pl.ds(q0, t), :]

            def tile(j, st, add=None, qblk=qblk, ka=ka):
                m, l, acc = st
                k0 = pl.multiple_of(j * t, t)
                s = _dot_nt(qblk, ka[pl.ds(k0, t), :])
                if add is not None:
                    s = s + add
                m_new = jnp.maximum(m, jnp.max(s, axis=-1, keepdims=True))
                a = jnp.exp(m - m_new)
                p = jnp.exp(s - m_new)
                l = a * l + jnp.sum(p, axis=-1, keepdims=True)
                acc = a * acc + _dot(p.astype(BF16), vb[pl.ds(k0, t), :])
                return m_new, l, acc

            st = (jnp.full((t, 1), NEG, F32), jnp.zeros((t, 1), F32), jnp.zeros((t, HEAD_W), F32))
            st = lax.fori_loop(0, sb + 1, tile, st)
            _, l, acc = tile(sb + 1, st, add=diag_add)
            res.append(acc / l)
        o = res[0] - lam * res[1]
        o = o * lax.rsqrt(jnp.mean(o * o, axis=-1, keepdims=True) + LN_EPS)
        o_ref[pl.ds(q0, t), :] = (o * sw_ref[...]).astype(BF16)
        return carry

    lax.fori_loop(0, n_sub, sub_block, 0)


def _prompt_attention(layer, slopes, lam, q, k_out, v_out, sw, bsz, seq):
    n, aw = q.shape
    depth, _, l, _ = k_out.shape
    n_heads = aw // HEAD_W
    key_rows = ATT_T + seq
    smem = pl.BlockSpec(memory_space=pltpu.SMEM)
    q_spec = pl.BlockSpec((pl.Element(seq), pl.Element(HEAD_W)),
                          lambda b, h: (pl.multiple_of(TM + b * seq, TM), pl.multiple_of(h * HEAD_W, HEAD_W)))
    kv_spec = pl.BlockSpec((None, None, l, HEAD_W), lambda b, h: (layer, b, 0, h))
    return pl.pallas_call(
        _attn_body, grid=(bsz, n_heads),
        in_specs=[smem, smem, q_spec, kv_spec, kv_spec, pl.BlockSpec((1, HEAD_W), lambda b, h: (0, 0))],
        out_specs=q_spec,
        out_shape=jax.ShapeDtypeStruct((n, aw), BF16),
        scratch_shapes=[pltpu.VMEM((key_rows, HEAD_W), BF16)] * 3 + [pltpu.VMEM((seq, HEAD_W), BF16)] * 2,
        compiler_params=_cparams("arbitrary", "arbitrary"), name="prompt_attention",
    )(slopes, lam, q, k_out, v_out, sw)


def _pool_mix(ext_ref, n_rows, counts, wp_ref, ps_ref):
    gw = wp_ref.shape[-1]
    outs = []
    for g, w in enumerate(POOL_WINDOWS):
        cols = slice(g * gw, (g + 1) * gw)
        tok = ext_ref[POOL_HALO:POOL_HALO + n_rows, cols]
        acc = tok
        for i in range(1, w):
            acc = acc + ext_ref[POOL_HALO - i:POOL_HALO - i + n_rows, cols]
        if counts is None:
            d = acc * (1.0 / w) - tok
        else:
            d = acc / jnp.minimum(counts, float(w)) - tok
        outs.append(_dot(d.astype(BF16), wp_ref[g]))
    return jnp.concatenate(outs, axis=-1) * ps_ref[...]


def _route(h1, r1_ref, rb_ref, cls_ref, ga_ref, gb_ref):
    hi = h1.astype(BF16)
    lo = (h1 - hi.astype(F32)).astype(BF16)
    at = _dot(hi, r1_ref[...]).T
    bt = _dot(lo, r1_ref[...]).T
    logit = at[0:N_EXPERTS] + at[N_EXPERTS:2 * N_EXPERTS] + bt[0:N_EXPERTS]
    lg = [logit[e:e + 1, :] for e in range(N_EXPERTS)]
    m = functools.reduce(jnp.maximum, lg)
    ex = [jnp.exp(x - m) for x in lg]
    z = functools.reduce(jnp.add, ex)
    prob = [x / z for x in ex]
    sel = [prob[e] + rb_ref[e] for e in range(N_EXPERTS)]

    def top2_sum(v):
        best = None
        for a, b in PAIRS:
            s = v[a] + v[b]
            best = s if best is None else jnp.maximum(best, s)
        return best

    score = [top2_sum(sel[g * EPG:(g + 1) * EPG]) for g in range(N_GROUPS)]
    gi = jnp.zeros_like(score[0], dtype=I32)
    best = score[0]
    for g in range(1, N_GROUPS):
        better = score[g] > best
        gi = jnp.where(better, g, gi)
        best = jnp.where(better, score[g], best)

    def pick(vals, j):
        out = vals[j]
        for g in range(1, N_GROUPS):
            out = jnp.where(gi == g, vals[g * EPG + j], out)
        return out

    sg = [pick(sel, j) for j in range(EPG)]
    pg = [pick(prob, j) for j in range(EPG)]

    def argmax_first(v, skip=None):
        bi = jnp.zeros_like(gi)
        bv = None
        for j in range(EPG):
            x = v[j] if skip is None else jnp.where(skip == j, -jnp.inf, v[j])
            if bv is None:
                bv = x
            else:
                better = x > bv
                bi = jnp.where(better, j, bi)
                bv = jnp.where(better, x, bv)
        return bi

    e1 = argmax_first(sg)
    e2 = argmax_first(sg, skip=e1)

    def at_idx(v, idx):
        out = v[0]
        for j in range(1, EPG):
            out = jnp.where(idx == j, v[j], out)
        return out

    g1 = at_idx(pg, e1)
    g2 = at_idx(pg, e2)
    tot = g1 + g2
    g1 = g1 / tot
    g2 = g2 / tot
    swap = e2 < e1
    ea = jnp.where(swap, e2, e1)
    eb = jnp.where(swap, e1, e2)
    pair = jnp.zeros_like(gi)
    for p, (a, b) in enumerate(PAIRS):
        pair = jnp.where((ea == a) & (eb == b), p, pair)
    cls_ref[...] = gi * len(PAIRS) + pair
    ga_ref[...] = jnp.where(swap, g2, g1)
    gb_ref[...] = jnp.where(swap, g1, g2)


def _post(alpha, mix, h, wo_ref, g_ref, b_ref, r1_ref, rb_ref, h1_ref, h1b_ref, cls_ref, ga_ref, gb_ref):
    h1 = _ln(alpha * h + _dot(mix, wo_ref[...]), g_ref[...], b_ref[...])
    h1_ref[...] = h1
    h1b_ref[...] = h1.astype(BF16)
    _route(h1, r1_ref, rb_ref, cls_ref, ga_ref, gb_ref)


def _post_out_shapes(n, d):
    nt = n // TM
    return [jax.ShapeDtypeStruct((n, d), F32), jax.ShapeDtypeStruct((n, d), BF16),
            jax.ShapeDtypeStruct((nt, 1, TM), I32), jax.ShapeDtypeStruct((nt, 1, TM), F32),
            jax.ShapeDtypeStruct((nt, 1, TM), F32)]


def _kx_body(alpha, n_dec, t_dec, slope_ref, lam_ref, rb_ref, q_ref, kx_ref, vx_ref, ux_ref, ck_ref, cv_ref,
             sp_ref, sw_ref, wp_ref, ps_ref, h_ref, wo_ref, g_ref, b_ref, r1_ref,
             h1_ref, h1b_ref, cls_ref, ga_ref, gb_ref, npool_ref, mix, ext):
    s = pl.program_id(0)
    past = ck_ref.shape[0]
    aw = q_ref.shape[1]
    n_heads = aw // HEAD_W
    is_meta = s == n_dec
    lam = lam_ref[0]
    r0 = pl.multiple_of(s * t_dec, t_dec)

    @pl.when(s == 0)
    def _():
        mix[...] = jnp.zeros_like(mix)

    q = q_ref[pl.ds(r0, t_dec), :].astype(F32)
    kn = kx_ref[pl.ds(r0, t_dec), :]
    vn = vx_ref[pl.ds(r0, t_dec), :]
    rq = lax.broadcasted_iota(I32, (t_dec, past), 0)
    cp = lax.broadcasted_iota(I32, (t_dec, past), 1)
    dist_p = (past + rq - cp).astype(F32)
    rn = lax.broadcasted_iota(I32, (t_dec, t_dec), 0)
    cn = lax.broadcasted_iota(I32, (t_dec, t_dec), 1)
    dist_n = jnp.abs(rn - cn).astype(F32)
    scale = HEAD_DIM ** -0.5
    heads = []
    for hd in range(n_heads):
        slope = slope_ref[hd]
        c0 = hd * HEAD_W
        ckh = ck_ref[:, c0:c0 + HEAD_W].astype(BF16)
        cvh = cv_ref[:, c0:c0 + HEAD_W].astype(BF16)
        maps = []
        for half in range(2):
            lo_c = c0 + half * HEAD_DIM
            qh = q[:, lo_c:lo_c + HEAD_DIM].astype(BF16)
            sp = _dot_nt(qh, ckh[:, half * HEAD_DIM:(half + 1) * HEAD_DIM]) * scale - slope * dist_p
            sp = jnp.where(is_meta, NEG, sp)
            sn = _dot_nt(qh, kn[:, lo_c:lo_c + HEAD_DIM].astype(BF16)) * scale - slope * dist_n
            m = jnp.maximum(jnp.max(sp, axis=-1, keepdims=True), jnp.max(sn, axis=-1, keepdims=True))
            pp = jnp.exp(sp - m)
            pn = jnp.exp(sn - m)
            z = jnp.sum(pp, axis=-1, keepdims=True) + jnp.sum(pn, axis=-1, keepdims=True)
            maps.append((pp / z, pn / z))
        wp_ = (maps[0][0] - lam * maps[1][0]).astype(BF16)
        wn_ = (maps[0][1] - lam * maps[1][1]).astype(BF16)
        o = _dot(wp_, cvh) + _dot(wn_, vn[:, c0:c0 + HEAD_W].astype(BF16))
        o = o * lax.rsqrt(jnp.mean(o * o, axis=-1, keepdims=True) + LN_EPS)
        heads.append(o * sw_ref[...])
    attn = jnp.concatenate(heads, axis=-1)

    hist = jnp.where(is_meta, 0.0, sp_ref[...])
    u = ux_ref[pl.ds(r0, t_dec), :]
    ext[POOL_HALO - sp_ref.shape[0]:POOL_HALO, :] = hist
    ext[POOL_HALO:POOL_HALO + t_dec, :] = u
    pos = lax.broadcasted_iota(I32, (t_dec, 1), 0) + jnp.where(is_meta, 0, past)
    pool = _pool_mix(ext, t_dec, (pos + 1).astype(F32), wp_ref, ps_ref)
    mix[pl.ds(r0, t_dec), :] = jnp.concatenate([attn, pool], axis=-1).astype(BF16)
    keep = sp_ref.shape[0]
    npool_ref[...] = ext[POOL_HALO + t_dec - keep:POOL_HALO + t_dec, :]

    @pl.when(s == n_dec)
    def _():
        _post(alpha, mix[...], h_ref[...], wo_ref, g_ref, b_ref, r1_ref, rb_ref,
              h1_ref, h1b_ref, cls_ref, ga_ref, gb_ref)


def _extras(layer, alpha, slopes, lam, rb, q, kx, vx, u, cache_k, cache_v, state_pool, sw, wp, ps, h, wo, g, b,
            r1, n_dec, t_dec):
    n, d = h.shape
    aw = q.shape[1]
    pw = u.shape[1]
    past = cache_k.shape[2]
    keep = state_pool.shape[2]
    smem = pl.BlockSpec(memory_space=pltpu.SMEM)
    tile0 = lambda w: pl.BlockSpec((TM, w), lambda s: (0, 0))
    full = lambda *shape: pl.BlockSpec(shape, lambda s: (0,) * len(shape))
    dec = lambda s: jnp.minimum(s, n_dec - 1)
    cache_spec = pl.BlockSpec((None, None, past, aw), lambda s: (layer, dec(s), 0, 0))
    small = pl.BlockSpec((None, 1, TM), lambda s: (0, 0, 0))
    out_shapes = _post_out_shapes(n, d) + [jax.ShapeDtypeStruct((n_dec + 1, keep, pw), F32)]
    out_specs = [tile0(d), tile0(d), small, small, small, pl.BlockSpec((None, keep, pw), lambda s: (s, 0, 0))]
    return pl.pallas_call(
        functools.partial(_kx_body, alpha, n_dec, t_dec), grid=(n_dec + 1,),
        in_specs=[smem, smem, smem, tile0(aw), full(TM, aw), full(TM, aw), tile0(pw), cache_spec, cache_spec,
                  pl.BlockSpec((None, None, keep, pw), lambda s: (layer, dec(s), 0, 0)),
                  full(1, HEAD_W), full(*wp.shape), full(1, pw), tile0(d), full(*wo.shape), full(1, d), full(1, d),
                  full(*r1.shape)],
        out_specs=out_specs, out_shape=out_shapes,
        scratch_shapes=[pltpu.VMEM((TM, aw + pw), BF16), pltpu.VMEM((POOL_HALO + t_dec, pw), F32)],
        compiler_params=_cparams("arbitrary"), name="extras_mix",
    )(slopes, lam, rb, q, kx, vx, u, cache_k, cache_v, state_pool, sw, wp, ps, h, wo, g, b, r1)


def _k3_body(alpha, tiles_per_b, rb_ref, a_ref, u_ref, uprev_ref, umeta_ref, wp_ref, ps_ref, h_ref, wo_ref, g_ref,
             b_ref, r1_ref, h1_in, h1b_in, cls_in, ga_in, gb_in, h1_ref, h1b_ref, cls_ref, ga_ref, gb_ref, ext):
    del h1_in, h1b_in, cls_in, ga_in, gb_in
    first = (pl.program_id(0) % tiles_per_b) == 0
    ext[0:POOL_HALO, :] = jnp.where(first, umeta_ref[...], uprev_ref[...])
    ext[POOL_HALO:, :] = u_ref[...]
    pool = _pool_mix(ext, TM, None, wp_ref, ps_ref)
    mix = jnp.concatenate([a_ref[...], pool.astype(BF16)], axis=-1)
    _post(alpha, mix, h_ref[...], wo_ref, g_ref, b_ref, r1_ref, rb_ref, h1_ref, h1b_ref, cls_ref, ga_ref, gb_ref)


def _frames_mix(alpha, rb, a, u, wp, ps, h, wo, g, b, r1, prev_outs, seq, meta_row):
    n, d = h.shape
    aw, pw = a.shape[1], u.shape[1]
    n_tiles = n // TM - 1
    tpb = seq // TM
    smem = pl.BlockSpec(memory_space=pltpu.SMEM)
    row = lambda w: pl.BlockSpec((TM, w), lambda i: (i + 1, 0))
    full = lambda *shape: pl.BlockSpec(shape, lambda i: (0,) * len(shape))
    hb = TM // POOL_HALO
    small = pl.BlockSpec((None, 1, TM), lambda i: (i + 1, 0, 0))
    any_spec = pl.BlockSpec(memory_space=pl.ANY)
    return pl.pallas_call(
        functools.partial(_k3_body, alpha, tpb), grid=(n_tiles,),
        in_specs=[smem, row(aw), row(pw),
                  pl.BlockSpec((POOL_HALO, pw), lambda i: ((i + 1) * hb - 1, 0)),
                  pl.BlockSpec((POOL_HALO, pw), lambda i: (meta_row // POOL_HALO, 0)),
                  full(*wp.shape), full(1, pw), row(d), full(*wo.shape), full(1, d), full(1, d), full(*r1.shape)]
                 + [any_spec] * 5,
        out_specs=[row(d), row(d), small, small, small],
        out_shape=_post_out_shapes(n, d),
        input_output_aliases={12: 0, 13: 1, 14: 2, 15: 3, 16: 4},
        scratch_shapes=[pltpu.VMEM((POOL_HALO + TM, pw), F32)],
        compiler_params=_cparams("arbitrary"), name="frames_mix",
    )(rb, a, u, u, u, wp, ps, h, wo, g, b, r1, *prev_outs)


def _moe_body(ea_ref, eb_ref, valid_ref, x_ref, ga_ref, gb_ref, wga, wua, wda, wgb, wub, wdb, y_ref):
    del ea_ref, eb_ref

    @pl.when(valid_ref[pl.program_id(0)] > 0)
    def _():
        x = x_ref[...]
        rows = x.shape[0]

        def lane_bcast(g_ref):
            return jnp.broadcast_to(g_ref[...], (HEAD_W, rows)).T

        def expert(wg, wu, wd, gate):
            gt = _dot(x, wg[...])
            hid = gt * (1.0 / (1.0 + jnp.exp(-gt))) * _dot(x, wu[...])
            reps = hid.shape[1] // HEAD_W
            hid = hid * jnp.concatenate([gate] * reps, axis=-1)
            return _dot(hid.astype(BF16), wd[...])

        y = expert(wga, wua, wda, lane_bcast(ga_ref))
        y_ref[...] = y + expert(wgb, wub, wdb, lane_bcast(gb_ref))

    @pl.when(valid_ref[pl.program_id(0)] == 0)
    def _():
        y_ref[...] = jnp.zeros_like(y_ref)


def _grouped_experts(ea, eb, valid, xs, gas, gbs, w_gate, w_up, w_down):
    n_pad, d = xs.shape
    n_tiles = n_pad // TMOE
    de = w_gate.shape[2]
    wa = lambda r, c: pl.BlockSpec((None, r, c), lambda t, ea, eb, valid: (ea[t], 0, 0))
    wb = lambda r, c: pl.BlockSpec((None, r, c), lambda t, ea, eb, valid: (eb[t], 0, 0))
    gate_spec = pl.BlockSpec((None, 1, TMOE), lambda t, ea, eb, valid: (t, 0, 0))
    grid_spec = pltpu.PrefetchScalarGridSpec(
        num_scalar_prefetch=3, grid=(n_tiles,),
        in_specs=[pl.BlockSpec((TMOE, d), lambda t, ea, eb, valid: (t, 0)), gate_spec, gate_spec,
                  wa(d, de), wa(d, de), wa(de, d), wb(d, de), wb(d, de), wb(de, d)],
        out_specs=pl.BlockSpec((TMOE, d), lambda t, ea, eb, valid: (t, 0)),
    )
    return pl.pallas_call(
        _moe_body, grid_spec=grid_spec, out_shape=jax.ShapeDtypeStruct((n_pad, d), F32),
        compiler_params=_cparams("arbitrary"), name="grouped_experts",
    )(ea, eb, valid, xs, gas.reshape(n_tiles, 1, TMOE), gbs.reshape(n_tiles, 1, TMOE),
      w_gate, w_up, w_down, w_gate, w_up, w_down)


def _dispatch_plan(cls):
    n = cls.shape[0]
    n_pad = n + N_CLASSES * TMOE
    onehot = (cls[:, None] == jnp.arange(N_CLASSES, dtype=I32)[None, :]).astype(I32)
    rank = jnp.take_along_axis(jnp.cumsum(onehot, axis=0), cls[:, None], axis=1)[:, 0] - 1
    counts = jnp.sum(onehot, axis=0)
    padded = (counts + TMOE - 1) // TMOE * TMOE
    ends = jnp.cumsum(padded)
    pos = (ends - padded)[cls] + rank
    tile_start = jnp.arange(n_pad // TMOE, dtype=I32) * TMOE
    tile_cls = jnp.minimum(jnp.searchsorted(ends, tile_start, side="right"), N_CLASSES - 1).astype(I32)
    valid = (tile_start < ends[-1]).astype(I32)
    pa = jnp.array([p[0] for p in PAIRS], I32)
    pb = jnp.array([p[1] for p in PAIRS], I32)
    grp = tile_cls // len(PAIRS)
    ea = grp * EPG + pa[tile_cls % len(PAIRS)]
    eb = grp * EPG + pb[tile_cls % len(PAIRS)]
    return pos.astype(I32), n_pad, ea.astype(I32), eb.astype(I32), valid


def _moe(h1b, cls, ga, gb, w_gate, w_up, w_down):
    pos, n_pad, ea, eb, valid = _dispatch_plan(cls)
    n, d = h1b.shape
    xs = jnp.zeros((n_pad, d), BF16).at[pos].set(h1b)
    gas = jnp.zeros((n_pad,), F32).at[pos].set(ga)
    gbs = jnp.zeros((n_pad,), F32).at[pos].set(gb)
    ys = _grouped_experts(ea, eb, valid, xs, gas, gbs, w_gate, w_up, w_down)
    return jnp.take(ys, pos, axis=0)


def _tail_body(alpha, h1_ref, y_ref, g_ref, b_ref, of_ref, ox_ref):
    out = _ln(alpha * h1_ref[...] + y_ref[...].astype(F32), g_ref[...], b_ref[...])
    of_ref[...] = out

    @pl.when(pl.program_id(0) == 0)
    def _():
        ox_ref[...] = out


def _tail(alpha, h1, y, g, b):
    n, d = h1.shape
    row = pl.BlockSpec((TM, d), lambda i: (i, 0))
    const = pl.BlockSpec((1, d), lambda i: (0, 0))
    return pl.pallas_call(
        functools.partial(_tail_body, alpha), grid=(n // TM,),
        in_specs=[row, row, const, const],
        out_specs=[pl.BlockSpec((TM, d), lambda i: (jnp.maximum(i - 1, 0), 0)),
                   pl.BlockSpec((TM, d), lambda i: (0, 0))],
        out_shape=[jax.ShapeDtypeStruct((n - TM, d), F32), jax.ShapeDtypeStruct((TM, d), F32)],
        compiler_params=_cparams("arbitrary"), name="final_norm",
    )(h1, y, g, b)


def kernel(x_prompt, x_sample, cache_k, cache_v, state_pool, meta_tokens, ln_in_g, ln_in_b, w_in, lambda_qk,
           subln_w, w_pool, pool_scale, w_out, ln1_g, ln1_b, router_w, router_b, w_gate, w_up, w_down,
           ln2_g, ln2_b):
    bsz, seq, d = x_prompt.shape
    n_dec, t_dec, _ = x_sample.shape
    depth = w_in.shape[0]
    past = cache_k.shape[2]
    n_heads = cache_k.shape[3]
    aw = n_heads * HEAD_W
    alpha = (2 * depth) ** 0.25
    n_sample = n_dec * t_dec
    assert seq % TM == 0 and n_sample + N_META <= TM and t_dec == N_META and n_sample % POOL_HALO == 0
    assert past % CHUNK == 0 and t_dec <= CHUNK

    x_extra = jnp.concatenate([x_sample.reshape(n_sample, d), meta_tokens.astype(F32),
                               jnp.zeros((TM - n_sample - N_META, d), F32)], axis=0)
    x_frames = x_prompt.reshape(bsz * seq, d)
    row = lambda v: v.reshape(1, -1).astype(F32)
    slopes = (2.0 ** (-8.0 * jnp.arange(1, n_heads + 1, dtype=F32) / n_heads)).astype(F32)
    rw_hi = router_w.astype(BF16)
    rw_lo = (router_w - rw_hi.astype(F32)).astype(BF16)
    r1 = jnp.concatenate([rw_hi, rw_lo, jnp.zeros((d, HEAD_W - 2 * N_EXPERTS), BF16)], axis=1)
    rb = router_b.astype(F32)
    ck = cache_k.reshape(depth, n_dec, past, aw)
    cv = cache_v.reshape(depth, n_dec, past, aw)

    k_out = v_out = None
    h1 = y = None
    pools_p, pools_s, ks_s, vs_s = [], [], [], []
    for l in range(depth):
        lam_init = 0.8 - 0.6 * math.exp(-0.3 * l)
        lq = lambda_qk[l].astype(F32)
        lam = (jnp.exp(jnp.sum(lq[0] * lq[1])) - jnp.exp(jnp.sum(lq[2] * lq[3])) + lam_init).reshape(1)
        sw = row(subln_w[l]) * (1.0 - lam_init)
        w_in_l = w_in[l].astype(BF16)
        if l == 0:
            h, q, u, k_out, v_out, kx, vx = _k1_first(x_extra, x_frames, row(ln_in_g), row(ln_in_b), w_in_l,
                                                       depth, bsz, seq)
        else:
            h, q, u, k_out, v_out, kx, vx = _k1_next(l, alpha, h1, y, row(ln2_g[l - 1]), row(ln2_b[l - 1]), w_in_l,
                                                      k_out, v_out, seq)
        meta_k = jnp.broadcast_to(kx[n_sample:n_sample + N_META][None], (bsz, N_META, aw))
        meta_v = jnp.broadcast_to(vx[n_sample:n_sample + N_META][None], (bsz, N_META, aw))
        k_out = k_out.at[l, :, :N_META, :].set(meta_k)
        v_out = v_out.at[l, :, :N_META, :].set(meta_v)
        a = _prompt_attention(l, slopes, lam, q, k_out, v_out, sw, bsz, seq)
        wp = w_pool[l].astype(BF16)
        ps = row(pool_scale[l])
        wo = w_out[l].astype(BF16)
        g1, b1 = row(ln1_g[l]), row(ln1_b[l])
        *x_outs, npool = _extras(l, alpha, slopes, lam, rb, q, kx, vx, u, ck, cv, state_pool, sw, wp, ps, h, wo,
                                 g1, b1, r1, n_dec, t_dec)
        h1, h1b, cls, ga, gb = _frames_mix(alpha, rb, a, u, wp, ps, h, wo, g1, b1, r1, x_outs, seq, n_sample)
        y = _moe(h1b, cls.reshape(-1), ga.reshape(-1), gb.reshape(-1),
                 w_gate[l].astype(BF16), w_up[l].astype(BF16), w_down[l].astype(BF16))
        keep = state_pool.shape[2]
        pools_p.append(u[TM:].reshape(bsz, seq, -1)[:, seq - keep:])
        pools_s.append(npool[:n_dec])
        ks_s.append(kx[:n_sample].reshape(n_dec, t_dec, n_heads, HEAD_W))
        vs_s.append(vx[:n_sample].reshape(n_dec, t_dec, n_heads, HEAD_W))

    y_frames, y_extra = _tail(alpha, h1, y, row(ln2_g[depth - 1]), row(ln2_b[depth - 1]))
    l_tot = N_META + seq
    return (y_frames.reshape(bsz, seq, d), y_extra[:n_sample].reshape(n_dec, t_dec, d),
            k_out.reshape(depth, bsz, l_tot, n_heads, HEAD_W), v_out.reshape(depth, bsz, l_tot, n_heads, HEAD_W),
            jnp.stack(pools_p), jnp.stack(ks_s), jnp.stack(vs_s), jnp.stack(pools_s))
```

```python
import functools
import math

import jax
import jax.numpy as jnp
from jax import lax
from jax.experimental import pallas as pl
from jax.experimental.pallas import tpu as pltpu

F32, BF16, I32 = jnp.float32, jnp.bfloat16, jnp.int32

CHUNK = 64
N_META = 16
HEAD_DIM = 64
HEAD_W = 2 * HEAD_DIM
POOL_WINDOWS = (2, 4, 8, 16)
POOL_HALO = 16
N_EXPERTS = 16
N_GROUPS = 4
EPG = N_EXPERTS // N_GROUPS
PAIRS = ((0, 1), (0, 2), (0, 3), (1, 2), (1, 3), (2, 3))
N_CLASSES = N_GROUPS * len(PAIRS)
LN_EPS = 1e-5
NEG = -1e30
LOG2E = 1.4426950408889634
Q_SCALE = HEAD_DIM ** -0.5 * LOG2E

TM = 512
TMOE = 256
ATT_T = 256
VMEM_LIMIT = 56 * 1024 * 1024


def _cparams(*sem):
    return pltpu.CompilerParams(dimension_semantics=sem, vmem_limit_bytes=VMEM_LIMIT)


def _ln(x, g, b):
    mu = jnp.mean(x, axis=-1, keepdims=True)
    xc = x - mu
    var = jnp.mean(xc * xc, axis=-1, keepdims=True)
    return xc * lax.rsqrt(var + LN_EPS) * g + b


def _dot(a, b):
    return jnp.dot(a, b, preferred_element_type=F32)


def _dot_nt(a, b):
    return lax.dot_general(a, b, (((1,), (1,)), ((), ())), preferred_element_type=F32)


def _k1_project(h, w_ref, h_ref, q_ref, u_ref, kb_ref, vb_ref, k_ref, v_ref, kx_ref, vx_ref, is_extras):
    aw = kx_ref.shape[-1]
    h_ref[...] = h
    hb = h.astype(BF16)
    q_ref[...] = (_dot(hb, w_ref[:, 0:aw]) * Q_SCALE).astype(BF16)
    k = _dot(hb, w_ref[:, aw:2 * aw])
    v = _dot(hb, w_ref[:, 2 * aw:3 * aw])
    u_ref[...] = _dot(hb, w_ref[:, 3 * aw:])
    kb_ref[...] = k.astype(BF16)
    vb_ref[...] = v.astype(BF16)
    k_ref[...] = k.reshape(k_ref.shape)
    v_ref[...] = v.reshape(v_ref.shape)
    if is_extras:
        kx_ref[...] = k
        vx_ref[...] = v


def _k1_first_body(xe_ref, x_ref, g_ref, b_ref, w_ref, h_ref, q_ref, u_ref, kb_ref, vb_ref, k_ref, v_ref, kx_ref,
                   vx_ref):
    outs = (w_ref, h_ref, q_ref, u_ref, kb_ref, vb_ref, k_ref, v_ref, kx_ref, vx_ref)

    @pl.when(pl.program_id(0) == 0)
    def _():
        _k1_project(_ln(xe_ref[...], g_ref[...], b_ref[...]), *outs, True)

    @pl.when(pl.program_id(0) > 0)
    def _():
        _k1_project(_ln(x_ref[...], g_ref[...], b_ref[...]), *outs, False)


def _k1_next_body(alpha, h1_ref, y_ref, g_ref, b_ref, w_ref, ko_ref, vo_ref,
                  h_ref, q_ref, u_ref, kb_ref, vb_ref, k_ref, v_ref, kx_ref, vx_ref):
    del ko_ref, vo_ref
    outs = (w_ref, h_ref, q_ref, u_ref, kb_ref, vb_ref, k_ref, v_ref, kx_ref, vx_ref)
    h = _ln(alpha * h1_ref[...] + y_ref[...].astype(F32), g_ref[...], b_ref[...])

    @pl.when(pl.program_id(0) == 0)
    def _():
        _k1_project(h, *outs, True)

    @pl.when(pl.program_id(0) > 0)
    def _():
        _k1_project(h, *outs, False)


def _k1_specs(layer, n_tiles, tiles_per_b, d, aw, pw):
    def frames_tile(i):
        return jnp.maximum(i - 1, 0)

    def kv_map(i):
        t = frames_tile(i)
        return (layer, t // tiles_per_b, pl.multiple_of(N_META + (t % tiles_per_b) * TM, 16), 0, 0)

    kv_spec = pl.BlockSpec((None, None, pl.Element(TM), pl.Element(aw // HEAD_W), pl.Element(HEAD_W)), kv_map)
    row = lambda w: pl.BlockSpec((TM, w), lambda i: (i, 0))
    const = lambda r, w: pl.BlockSpec((r, w), lambda i: (0, 0))
    out_specs = [row(d), row(aw), row(pw), row(aw), row(aw), kv_spec, kv_spec, const(TM, aw), const(TM, aw)]
    return row, const, frames_tile, out_specs


def _k1_out_shapes(n, d, aw, pw, depth, b, l):
    return [
        jax.ShapeDtypeStruct((n, d), F32),
        jax.ShapeDtypeStruct((n, aw), BF16),
        jax.ShapeDtypeStruct((n, pw), F32),
        jax.ShapeDtypeStruct((n, aw), BF16),
        jax.ShapeDtypeStruct((n, aw), BF16),
        jax.ShapeDtypeStruct((depth, b, l, aw // HEAD_W, HEAD_W), F32),
        jax.ShapeDtypeStruct((depth, b, l, aw // HEAD_W, HEAD_W), F32),
        jax.ShapeDtypeStruct((TM, aw), F32),
        jax.ShapeDtypeStruct((TM, aw), F32),
    ]


def _k1_first(x_extra, x_frames, g, b, w, depth, bsz, seq):
    nf, d = x_frames.shape
    aw = w.shape[1] // 4
    pw = w.shape[1] - 3 * aw
    n = TM + nf
    n_tiles = n // TM
    tpb = seq // TM
    row, const, frames_tile, out_specs = _k1_specs(0, n_tiles, tpb, d, aw, pw)
    in_specs = [const(TM, d), pl.BlockSpec((TM, d), lambda i: (frames_tile(i), 0)),
                const(1, d), const(1, d), const(d, w.shape[1])]
    return pl.pallas_call(
        _k1_first_body, grid=(n_tiles,), in_specs=in_specs, out_specs=out_specs,
        out_shape=_k1_out_shapes(n, d, aw, pw, depth, bsz, N_META + seq),
        compiler_params=_cparams("arbitrary"), name="k1_first",
    )(x_extra, x_frames, g, b, w)


def _k1_next(layer, alpha, h1, y, g, b, w, k_out, v_out, seq):
    n, d = h1.shape
    depth, bsz, l, n_heads, _ = k_out.shape
    aw = n_heads * HEAD_W
    pw = w.shape[1] - 3 * aw
    n_tiles = n // TM
    tpb = seq // TM
    row, const, _, out_specs = _k1_specs(layer, n_tiles, tpb, d, aw, pw)
    any_spec = pl.BlockSpec(memory_space=pl.ANY)
    in_specs = [row(d), row(d), const(1, d), const(1, d), const(d, w.shape[1]), any_spec, any_spec]
    return pl.pallas_call(
        functools.partial(_k1_next_body, alpha), grid=(n_tiles,), in_specs=in_specs, out_specs=out_specs,
        out_shape=_k1_out_shapes(n, d, aw, pw, depth, bsz, l),
        input_output_aliases={5: 5, 6: 6},
        compiler_params=_cparams("arbitrary"), name="k1_next",
    )(h1, y, g, b, w, k_out, v_out)


def _bf16_part(x):
    bits = lax.bitcast_convert_type(x, jnp.uint32) & jnp.uint32(0xFFFF0000)
    return lax.bitcast_convert_type(bits, F32)


def _split3(c):
    hi = _bf16_part(c)
    r = c - hi
    mid = _bf16_part(r)
    return hi, mid, r - mid


def _alibi_tables(slopes, seq):
    pad = ATT_T - N_META
    pos = jnp.concatenate([jnp.zeros((pad,), F32), jnp.arange(N_META + seq, dtype=F32)])
    live = jnp.arange(ATT_T + seq) >= pad
    c = jnp.where(live[None, :], (slopes * LOG2E)[:, None] * pos[None, :], NEG)
    bias = jnp.pad(jnp.stack(_split3(c), axis=-1), ((0, 0), (0, 0), (0, HEAD_W - 3))).astype(BF16)
    rr = lax.broadcasted_iota(I32, (ATT_T, ATT_T), 0)
    cc = lax.broadcasted_iota(I32, (ATT_T, ATT_T), 1)
    ahead = jnp.maximum(cc - rr, 0).astype(F32)
    diag = jnp.where(((cc // CHUNK) <= (rr // CHUNK))[None], (-2.0 * LOG2E * slopes)[:, None, None] * ahead[None], NEG)
    return bias, diag


def _attn_body(lam_ref, q_ref, kb_ref, vb_ref, kx_ref, vx_ref, bias_ref, diag_ref, sw_ref, o_ref, ka, vm, qa1, qa2):
    t = ATT_T
    lam = lam_ref[0]
    s_len = q_ref.shape[0]
    n_sub = s_len // t
    pad = t - N_META
    lane = lax.broadcasted_iota(I32, (1, HEAD_W), 1)
    first_half = (lane < HEAD_DIM).astype(BF16)
    ones = jnp.broadcast_to((lane < 3).astype(BF16), (s_len, HEAD_W))

    ka[0:pad, 0:HEAD_W] = jnp.zeros((pad, HEAD_W), BF16)
    ka[pad:t, 0:HEAD_W] = kx_ref[...].astype(BF16)
    ka[t:, 0:HEAD_W] = kb_ref[...]
    ka[:, HEAD_W:] = bias_ref[...]
    vm[0:pad, :] = jnp.zeros((pad, HEAD_W), BF16)
    vm[pad:t, :] = vx_ref[...].astype(BF16)
    qb = q_ref[...]
    qa1[:, 0:HEAD_W] = qb * first_half
    qa2[:, 0:HEAD_W] = qb * (1 - first_half)
    qa1[:, HEAD_W:] = ones
    qa2[:, HEAD_W:] = ones

    def scores(sb):
        lk = (sb + 2) * t
        out = []
        for qa in (qa1, qa2):
            s = _dot_nt(qa[sb * t:(sb + 1) * t, :], ka[0:lk, :])
            out.append(jnp.concatenate([s[:, :lk - t], s[:, lk - t:] + diag_ref[...]], axis=-1))
        return out

    nxt = scores(0)
    for sb in range(n_sub):
        q0 = sb * t
        lk = (sb + 2) * t
        cur = nxt
        if sb + 1 < n_sub:
            nxt = scores(sb + 1)
        ps, ls = [], []
        for s in cur:
            p = jnp.exp2(s - jnp.max(s, axis=-1, keepdims=True))
            ps.append(p)
            ls.append(jnp.sum(p, axis=-1, keepdims=True))
        w = (ps[0] - ps[1] * (lam * ls[0] / ls[1])).astype(BF16)
        o = (_dot(w[:, :t], vm[...]) + _dot(w[:, t:], vb_ref[0:lk - t, :])) / ls[0]
        o = o * lax.rsqrt(jnp.mean(o * o, axis=-1, keepdims=True) + LN_EPS)
        o_ref[q0:q0 + t, :] = (o * sw_ref[...]).astype(BF16)


def _prompt_attention(lam, q, kb, vb, kx, vx, bias, diag, sw, bsz, seq, meta_row):
    n, aw = q.shape
    n_heads = aw // HEAD_W
    key_rows = ATT_T + seq
    smem = pl.BlockSpec(memory_space=pltpu.SMEM)
    frames = pl.BlockSpec((pl.Element(seq), pl.Element(HEAD_W)),
                          lambda b, h: (pl.multiple_of(TM + b * seq, TM), pl.multiple_of(h * HEAD_W, HEAD_W)))
    meta = pl.BlockSpec((N_META, HEAD_W), lambda b, h: (meta_row // N_META, h))
    return pl.pallas_call(
        _attn_body, grid=(bsz, n_heads),
        in_specs=[smem, frames, frames, frames, meta, meta,
                  pl.BlockSpec((None, key_rows, HEAD_W), lambda b, h: (h, 0, 0)),
                  pl.BlockSpec((None, ATT_T, ATT_T), lambda b, h: (h, 0, 0)),
                  pl.BlockSpec((1, HEAD_W), lambda b, h: (0, 0))],
        out_specs=frames,
        out_shape=jax.ShapeDtypeStruct((n, aw), BF16),
        scratch_shapes=[pltpu.VMEM((key_rows, 2 * HEAD_W), BF16), pltpu.VMEM((ATT_T, HEAD_W), BF16),
                        pltpu.VMEM((seq, 2 * HEAD_W), BF16), pltpu.VMEM((seq, 2 * HEAD_W), BF16)],
        compiler_params=_cparams("arbitrary", "arbitrary"), name="prompt_attention",
    )(lam, q, kb, vb, kx, vx, bias, diag, sw)


def _pool_mix(ext_ref, n_rows, counts, wp_ref, ps_ref):
    gw = wp_ref.shape[-1]
    outs = []
    for g, w in enumerate(POOL_WINDOWS):
        cols = slice(g * gw, (g + 1) * gw)
        tok = ext_ref[POOL_HALO:POOL_HALO + n_rows, cols]
        acc = tok
        for i in range(1, w):
            acc = acc + ext_ref[POOL_HALO - i:POOL_HALO - i + n_rows, cols]
        if counts is None:
            d = acc * (1.0 / w) - tok
        else:
            d = acc / jnp.minimum(counts, float(w)) - tok
        outs.append(_dot(d.astype(BF16), wp_ref[g]))
    return jnp.concatenate(outs, axis=-1) * ps_ref[...]


def _route(h1, r1_ref, rb_ref, cls_ref, ga_ref, gb_ref):
    hi = h1.astype(BF16)
    lo = (h1 - hi.astype(F32)).astype(BF16)
    at = _dot(hi, r1_ref[...]).T
    bt = _dot(lo, r1_ref[...]).T
    logit = at[0:N_EXPERTS] + at[N_EXPERTS:2 * N_EXPERTS] + bt[0:N_EXPERTS]
    lg = [logit[e:e + 1, :] for e in range(N_EXPERTS)]
    m = functools.reduce(jnp.maximum, lg)
    ex = [jnp.exp(x - m) for x in lg]
    z = functools.reduce(jnp.add, ex)
    prob = [x / z for x in ex]
    sel = [prob[e] + rb_ref[e] for e in range(N_EXPERTS)]

    def top2_sum(v):
        best = None
        for a, b in PAIRS:
            s = v[a] + v[b]
            best = s if best is None else jnp.maximum(best, s)
        return best

    score = [top2_sum(sel[g * EPG:(g + 1) * EPG]) for g in range(N_GROUPS)]
    gi = jnp.zeros_like(score[0], dtype=I32)
    best = score[0]
    for g in range(1, N_GROUPS):
        better = score[g] > best
        gi = jnp.where(better, g, gi)
        best = jnp.where(better, score[g], best)

    def pick(vals, j):
        out = vals[j]
        for g in range(1, N_GROUPS):
            out = jnp.where(gi == g, vals[g * EPG + j], out)
        return out

    sg = [pick(sel, j) for j in range(EPG)]
    pg = [pick(prob, j) for j in range(EPG)]

    def argmax_first(v, skip=None):
        bi = jnp.zeros_like(gi)
        bv = None
        for j in range(EPG):
            x = v[j] if skip is None else jnp.where(skip == j, -jnp.inf, v[j])
            if bv is None:
                bv = x
            else:
                better = x > bv
                bi = jnp.where(better, j, bi)
                bv = jnp.where(better, x, bv)
        return bi

    e1 = argmax_first(sg)
    e2 = argmax_first(sg, skip=e1)

    def at_idx(v, idx):
        out = v[0]
        for j in range(1, EPG):
            out = jnp.where(idx == j, v[j], out)
        return out

    g1 = at_idx(pg, e1)
    g2 = at_idx(pg, e2)
    tot = g1 + g2
    g1 = g1 / tot
    g2 = g2 / tot
    swap = e2 < e1
    ea = jnp.where(swap, e2, e1)
    eb = jnp.where(swap, e1, e2)
    pair = jnp.zeros_like(gi)
    for p, (a, b) in enumerate(PAIRS):
        pair = jnp.where((ea == a) & (eb == b), p, pair)
    cls_ref[...] = gi * len(PAIRS) + pair
    ga_ref[...] = jnp.where(swap, g2, g1)
    gb_ref[...] = jnp.where(swap, g1, g2)


def _post(alpha, mix, h, wo_ref, g_ref, b_ref, r1_ref, rb_ref, h1_ref, h1b_ref, cls_ref, ga_ref, gb_ref):
    h1 = _ln(alpha * h + _dot(mix, wo_ref[...]), g_ref[...], b_ref[...])
    h1_ref[...] = h1
    h1b_ref[...] = h1.astype(BF16)
    _route(h1, r1_ref, rb_ref, cls_ref, ga_ref, gb_ref)


def _post_out_shapes(n, d):
    nt = n // TM
    return [jax.ShapeDtypeStruct((n, d), F32), jax.ShapeDtypeStruct((n, d), BF16),
            jax.ShapeDtypeStruct((nt, 1, TM), I32), jax.ShapeDtypeStruct((nt, 1, TM), F32),
            jax.ShapeDtypeStruct((nt, 1, TM), F32)]


def _kx_body(alpha, n_dec, t_dec, slope_ref, lam_ref, rb_ref, q_ref, kx_ref, vx_ref, ux_ref, ck_ref, cv_ref,
             sp_ref, sw_ref, wp_ref, ps_ref, h_ref, wo_ref, g_ref, b_ref, r1_ref,
             h1_ref, h1b_ref, cls_ref, ga_ref, gb_ref, npool_ref, mix, ext):
    s = pl.program_id(0)
    past = ck_ref.shape[0]
    aw = q_ref.shape[1]
    n_heads = aw // HEAD_W
    is_meta = s == n_dec
    lam = lam_ref[0]
    r0 = pl.multiple_of(s * t_dec, t_dec)

    @pl.when(s == 0)
    def _():
        mix[...] = jnp.zeros_like(mix)

    q = q_ref[pl.ds(r0, t_dec), :].astype(F32)
    kn = kx_ref[pl.ds(r0, t_dec), :]
    vn = vx_ref[pl.ds(r0, t_dec), :]
    rq = lax.broadcasted_iota(I32, (t_dec, past), 0)
    cp = lax.broadcasted_iota(I32, (t_dec, past), 1)
    dist_p = (past + rq - cp).astype(F32)
    rn = lax.broadcasted_iota(I32, (t_dec, t_dec), 0)
    cn = lax.broadcasted_iota(I32, (t_dec, t_dec), 1)
    dist_n = jnp.abs(rn - cn).astype(F32)
    heads = []
    for hd in range(n_heads):
        slope = slope_ref[hd] * LOG2E
        c0 = hd * HEAD_W
        ckh = ck_ref[:, c0:c0 + HEAD_W].astype(BF16)
        cvh = cv_ref[:, c0:c0 + HEAD_W].astype(BF16)
        maps = []
        for half in range(2):
            lo_c = c0 + half * HEAD_DIM
            qh = q[:, lo_c:lo_c + HEAD_DIM].astype(BF16)
            sp = _dot_nt(qh, ckh[:, half * HEAD_DIM:(half + 1) * HEAD_DIM]) - slope * dist_p
            sp = jnp.where(is_meta, NEG, sp)
            sn = _dot_nt(qh, kn[:, lo_c:lo_c + HEAD_DIM].astype(BF16)) - slope * dist_n
            m = jnp.maximum(jnp.max(sp, axis=-1, keepdims=True), jnp.max(sn, axis=-1, keepdims=True))
            pp = jnp.exp2(sp - m)
            pn = jnp.exp2(sn - m)
            z = jnp.sum(pp, axis=-1, keepdims=True) + jnp.sum(pn, axis=-1, keepdims=True)
            maps.append((pp / z, pn / z))
        wp_ = (maps[0][0] - lam * maps[1][0]).astype(BF16)
        wn_ = (maps[0][1] - lam * maps[1][1]).astype(BF16)
        o = _dot(wp_, cvh) + _dot(wn_, vn[:, c0:c0 + HEAD_W].astype(BF16))
        o = o * lax.rsqrt(jnp.mean(o * o, axis=-1, keepdims=True) + LN_EPS)
        heads.append(o * sw_ref[...])
    attn = jnp.concatenate(heads, axis=-1)

    hist = jnp.where(is_meta, 0.0, sp_ref[...])
    u = ux_ref[pl.ds(r0, t_dec), :]
    ext[POOL_HALO - sp_ref.shape[0]:POOL_HALO, :] = hist
    ext[POOL_HALO:POOL_HALO + t_dec, :] = u
    pos = lax.broadcasted_iota(I32, (t_dec, 1), 0) + jnp.where(is_meta, 0, past)
    pool = _pool_mix(ext, t_dec, (pos + 1).astype(F32), wp_ref, ps_ref)
    mix[pl.ds(r0, t_dec), :] = jnp.concatenate([attn, pool], axis=-1).astype(BF16)
    keep = sp_ref.shape[0]
    npool_ref[...] = ext[POOL_HALO + t_dec - keep:POOL_HALO + t_dec, :]

    @pl.when(s == n_dec)
    def _():
        _post(alpha, mix[...], h_ref[...], wo_ref, g_ref, b_ref, r1_ref, rb_ref,
              h1_ref, h1b_ref, cls_ref, ga_ref, gb_ref)


def _extras(layer, alpha, slopes, lam, rb, q, kx, vx, u, cache_k, cache_v, state_pool, sw, wp, ps, h, wo, g, b,
            r1, n_dec, t_dec):
    n, d = h.shape
    aw = q.shape[1]
    pw = u.shape[1]
    past = cache_k.shape[2]
    keep = state_pool.shape[2]
    smem = pl.BlockSpec(memory_space=pltpu.SMEM)
    tile0 = lambda w: pl.BlockSpec((TM, w), lambda s: (0, 0))
    full = lambda *shape: pl.BlockSpec(shape, lambda s: (0,) * len(shape))
    dec = lambda s: jnp.minimum(s, n_dec - 1)
    cache_spec = pl.BlockSpec((None, None, past, aw), lambda s: (layer, dec(s), 0, 0))
    small = pl.BlockSpec((None, 1, TM), lambda s: (0, 0, 0))
    out_shapes = _post_out_shapes(n, d) + [jax.ShapeDtypeStruct((n_dec + 1, keep, pw), F32)]
    out_specs = [tile0(d), tile0(d), small, small, small, pl.BlockSpec((None, keep, pw), lambda s: (s, 0, 0))]
    return pl.pallas_call(
        functools.partial(_kx_body, alpha, n_dec, t_dec), grid=(n_dec + 1,),
        in_specs=[smem, smem, smem, tile0(aw), full(TM, aw), full(TM, aw), tile0(pw), cache_spec, cache_spec,
                  pl.BlockSpec((None, None, keep, pw), lambda s: (layer, dec(s), 0, 0)),
                  full(1, HEAD_W), full(*wp.shape), full(1, pw), tile0(d), full(*wo.shape), full(1, d), full(1, d),
                  full(*r1.shape)],
        out_specs=out_specs, out_shape=out_shapes,
        scratch_shapes=[pltpu.VMEM((TM, aw + pw), BF16), pltpu.VMEM((POOL_HALO + t_dec, pw), F32)],
        compiler_params=_cparams("arbitrary"), name="extras_mix",
    )(slopes, lam, rb, q, kx, vx, u, cache_k, cache_v, state_pool, sw, wp, ps, h, wo, g, b, r1)


def _k3_body(alpha, tiles_per_b, rb_ref, a_ref, u_ref, uprev_ref, umeta_ref, wp_ref, ps_ref, h_ref, wo_ref, g_ref,
             b_ref, r1_ref, h1_in, h1b_in, cls_in, ga_in, gb_in, h1_ref, h1b_ref, cls_ref, ga_ref, gb_ref, ext):
    del h1_in, h1b_in, cls_in, ga_in, gb_in
    first = (pl.program_id(0) % tiles_per_b) == 0
    ext[0:POOL_HALO, :] = jnp.where(first, umeta_ref[...], uprev_ref[...])
    ext[POOL_HALO:, :] = u_ref[...]
    pool = _pool_mix(ext, TM, None, wp_ref, ps_ref)
    mix = jnp.concatenate([a_ref[...], pool.astype(BF16)], axis=-1)
    _post(alpha, mix, h_ref[...], wo_ref, g_ref, b_ref, r1_ref, rb_ref, h1_ref, h1b_ref, cls_ref, ga_ref, gb_ref)


def _frames_mix(alpha, rb, a, u, wp, ps, h, wo, g, b, r1, prev_outs, seq, meta_row):
    n, d = h.shape
    aw, pw = a.shape[1], u.shape[1]
    n_tiles = n // TM - 1
    tpb = seq // TM
    smem = pl.BlockSpec(memory_space=pltpu.SMEM)
    row = lambda w: pl.BlockSpec((TM, w), lambda i: (i + 1, 0))
    full = lambda *shape: pl.BlockSpec(shape, lambda i: (0,) * len(shape))
    hb = TM // POOL_HALO
    small = pl.BlockSpec((None, 1, TM), lambda i: (i + 1, 0, 0))
    any_spec = pl.BlockSpec(memory_space=pl.ANY)
    return pl.pallas_call(
        functools.partial(_k3_body, alpha, tpb), grid=(n_tiles,),
        in_specs=[smem, row(aw), row(pw),
                  pl.BlockSpec((POOL_HALO, pw), lambda i: ((i + 1) * hb - 1, 0)),
                  pl.BlockSpec((POOL_HALO, pw), lambda i: (meta_row // POOL_HALO, 0)),
                  full(*wp.shape), full(1, pw), row(d), full(*wo.shape), full(1, d), full(1, d), full(*r1.shape)]
                 + [any_spec] * 5,
        out_specs=[row(d), row(d), small, small, small],
        out_shape=_post_out_shapes(n, d),
        input_output_aliases={12: 0, 13: 1, 14: 2, 15: 3, 16: 4},
        scratch_shapes=[pltpu.VMEM((POOL_HALO + TM, pw), F32)],
        compiler_params=_cparams("arbitrary"), name="frames_mix",
    )(rb, a, u, u, u, wp, ps, h, wo, g, b, r1, *prev_outs)


def _moe_body(ea_ref, eb_ref, valid_ref, x_ref, ga_ref, gb_ref, wga, wua, wda, wgb, wub, wdb, y_ref):
    del ea_ref, eb_ref

    @pl.when(valid_ref[pl.program_id(0)] > 0)
    def _():
        x = x_ref[...]
        rows = x.shape[0]

        def lane_bcast(g_ref):
            return jnp.broadcast_to(g_ref[...], (HEAD_W, rows)).T

        def expert(wg, wu, wd, gate):
            gt = _dot(x, wg[...])
            hid = gt * (1.0 / (1.0 + jnp.exp(-gt))) * _dot(x, wu[...])
            reps = hid.shape[1] // HEAD_W
            hid = hid * jnp.concatenate([gate] * reps, axis=-1)
            return _dot(hid.astype(BF16), wd[...])

        y = expert(wga, wua, wda, lane_bcast(ga_ref))
        y_ref[...] = y + expert(wgb, wub, wdb, lane_bcast(gb_ref))

    @pl.when(valid_ref[pl.program_id(0)] == 0)
    def _():
        y_ref[...] = jnp.zeros_like(y_ref)


def _grouped_experts(ea, eb, valid, xs, gas, gbs, w_gate, w_up, w_down):
    n_pad, d = xs.shape
    n_tiles = n_pad // TMOE
    de = w_gate.shape[2]
    wa = lambda r, c: pl.BlockSpec((None, r, c), lambda t, ea, eb, valid: (ea[t], 0, 0))
    wb = lambda r, c: pl.BlockSpec((None, r, c), lambda t, ea, eb, valid: (eb[t], 0, 0))
    gate_spec = pl.BlockSpec((None, 1, TMOE), lambda t, ea, eb, valid: (t, 0, 0))
    grid_spec = pltpu.PrefetchScalarGridSpec(
        num_scalar_prefetch=3, grid=(n_tiles,),
        in_specs=[pl.BlockSpec((TMOE, d), lambda t, ea, eb, valid: (t, 0)), gate_spec, gate_spec,
                  wa(d, de), wa(d, de), wa(de, d), wb(d, de), wb(d, de), wb(de, d)],
        out_specs=pl.BlockSpec((TMOE, d), lambda t, ea, eb, valid: (t, 0)),
    )
    return pl.pallas_call(
        _moe_body, grid_spec=grid_spec, out_shape=jax.ShapeDtypeStruct((n_pad, d), F32),
        compiler_params=_cparams("arbitrary"), name="grouped_experts",
    )(ea, eb, valid, xs, gas.reshape(n_tiles, 1, TMOE), gbs.reshape(n_tiles, 1, TMOE),
      w_gate, w_up, w_down, w_gate, w_up, w_down)


def _rows(x, idx):
    return x.at[idx].get(mode="promise_in_bounds")


def _dispatch_plan(cls):
    n = cls.shape[0]
    n_pad = n + N_CLASSES * TMOE
    classes = jnp.arange(N_CLASSES, dtype=I32)
    onehot = (cls[:, None] == classes[None, :]).astype(I32)
    rank = jnp.sum(jnp.cumsum(onehot, axis=0) * onehot, axis=1) - 1
    counts = jnp.sum(onehot, axis=0)
    padded = (counts + TMOE - 1) // TMOE * TMOE
    ends = jnp.cumsum(padded)
    starts = ends - padded
    first = jnp.cumsum(counts) - counts
    pos = jnp.sum(onehot * starts[None, :], axis=1) + rank
    order = jnp.argsort(cls, stable=True).astype(I32)
    row = jnp.arange(n_pad, dtype=I32)
    row_cls = jnp.minimum(jnp.sum((ends[None, :] <= row[:, None]).astype(I32), axis=1), N_CLASSES - 1)
    row_hot = (row_cls[:, None] == classes[None, :]).astype(I32)
    within = row - jnp.sum(row_hot * starts[None, :], axis=1)
    live = within < jnp.sum(row_hot * counts[None, :], axis=1)
    src = _rows(order, jnp.clip(jnp.sum(row_hot * first[None, :], axis=1) + within, 0, n - 1))
    src = jnp.where(live, src, 0)
    tile_cls = row_cls[::TMOE]
    valid = (row[::TMOE] < ends[-1]).astype(I32)
    pa = jnp.array([p[0] for p in PAIRS], I32)
    pb = jnp.array([p[1] for p in PAIRS], I32)
    grp = tile_cls // len(PAIRS)
    pair_hot = ((tile_cls % len(PAIRS))[:, None] == jnp.arange(len(PAIRS), dtype=I32)[None, :]).astype(I32)
    ea = grp * EPG + jnp.sum(pair_hot * pa[None, :], axis=1)
    eb = grp * EPG + jnp.sum(pair_hot * pb[None, :], axis=1)
    return pos.astype(I32), src, live, ea.astype(I32), eb.astype(I32), valid


def _moe(h1b, cls, ga, gb, w_gate, w_up, w_down):
    pos, src, live, ea, eb, valid = _dispatch_plan(cls)
    xs = _rows(h1b, src)
    gas = jnp.where(live, _rows(ga, src), 0.0)
    gbs = jnp.where(live, _rows(gb, src), 0.0)
    ys = _grouped_experts(ea, eb, valid, xs, gas, gbs, w_gate, w_up, w_down)
    return _rows(ys, pos)


def _tail_body(alpha, h1_ref, y_ref, g_ref, b_ref, of_ref, ox_ref):
    out = _ln(alpha * h1_ref[...] + y_ref[...].astype(F32), g_ref[...], b_ref[...])
    of_ref[...] = out

    @pl.when(pl.program_id(0) == 0)
    def _():
        ox_ref[...] = out


def _tail(alpha, h1, y, g, b):
    n, d = h1.shape
    row = pl.BlockSpec((TM, d), lambda i: (i, 0))
    const = pl.BlockSpec((1, d), lambda i: (0, 0))
    return pl.pallas_call(
        functools.partial(_tail_body, alpha), grid=(n // TM,),
        in_specs=[row, row, const, const],
        out_specs=[pl.BlockSpec((TM, d), lambda i: (jnp.maximum(i - 1, 0), 0)),
                   pl.BlockSpec((TM, d), lambda i: (0, 0))],
        out_shape=[jax.ShapeDtypeStruct((n - TM, d), F32), jax.ShapeDtypeStruct((TM, d), F32)],
        compiler_params=_cparams("arbitrary"), name="final_norm",
    )(h1, y, g, b)


def kernel(x_prompt, x_sample, cache_k, cache_v, state_pool, meta_tokens, ln_in_g, ln_in_b, w_in, lambda_qk,
           subln_w, w_pool, pool_scale, w_out, ln1_g, ln1_b, router_w, router_b, w_gate, w_up, w_down,
           ln2_g, ln2_b):
    bsz, seq, d = x_prompt.shape
    n_dec, t_dec, _ = x_sample.shape
    depth = w_in.shape[0]
    past = cache_k.shape[2]
    n_heads = cache_k.shape[3]
    aw = n_heads * HEAD_W
    alpha = (2 * depth) ** 0.25
    n_sample = n_dec * t_dec
    assert seq % TM == 0 and n_sample + N_META <= TM and t_dec == N_META and n_sample % POOL_HALO == 0
    assert past % CHUNK == 0 and t_dec <= CHUNK

    x_extra = jnp.concatenate([x_sample.reshape(n_sample, d), meta_tokens.astype(F32),
                               jnp.zeros((TM - n_sample - N_META, d), F32)], axis=0)
    x_frames = x_prompt.reshape(bsz * seq, d)
    row = lambda v: v.reshape(1, -1).astype(F32)
    slopes = (2.0 ** (-8.0 * jnp.arange(1, n_heads + 1, dtype=F32) / n_heads)).astype(F32)
    bias, diag = _alibi_tables(slopes, seq)
    rw_hi = _bf16_part(router_w.astype(F32))
    rw_lo = router_w.astype(F32) - rw_hi
    r1 = jnp.concatenate([rw_hi, rw_lo, jnp.zeros((d, HEAD_W - 2 * N_EXPERTS), F32)], axis=1).astype(BF16)
    rb = router_b.astype(F32)
    ck = cache_k.reshape(depth, n_dec, past, aw)
    cv = cache_v.reshape(depth, n_dec, past, aw)

    k_out = v_out = None
    h1 = y = None
    pools_p, pools_s, ks_s, vs_s = [], [], [], []
    for l in range(depth):
        lam_init = 0.8 - 0.6 * math.exp(-0.3 * l)
        lq = lambda_qk[l].astype(F32)
        lam = (jnp.exp(jnp.sum(lq[0] * lq[1])) - jnp.exp(jnp.sum(lq[2] * lq[3])) + lam_init).reshape(1)
        sw = row(subln_w[l]) * (1.0 - lam_init)
        w_in_l = w_in[l].astype(BF16)
        if l == 0:
            k1_outs = _k1_first(x_extra, x_frames, row(ln_in_g), row(ln_in_b), w_in_l, depth, bsz, seq)
        else:
            k1_outs = _k1_next(l, alpha, h1, y, row(ln2_g[l - 1]), row(ln2_b[l - 1]), w_in_l, k_out, v_out, seq)
        h, q, u, kb, vb, k_out, v_out, kx, vx = k1_outs
        meta_shape = (bsz, N_META, n_heads, HEAD_W)
        meta_k = jnp.broadcast_to(kx[n_sample:n_sample + N_META].reshape(1, *meta_shape[1:]), meta_shape)
        meta_v = jnp.broadcast_to(vx[n_sample:n_sample + N_META].reshape(1, *meta_shape[1:]), meta_shape)
        k_out = k_out.at[l, :, :N_META].set(meta_k)
        v_out = v_out.at[l, :, :N_META].set(meta_v)
        a = _prompt_attention(lam, q, kb, vb, kx, vx, bias, diag, sw, bsz, seq, n_sample)
        wp = w_pool[l].astype(BF16)
        ps = row(pool_scale[l])
        wo = w_out[l].astype(BF16)
        g1, b1 = row(ln1_g[l]), row(ln1_b[l])
        *x_outs, npool = _extras(l, alpha, slopes, lam, rb, q, kx, vx, u, ck, cv, state_pool, sw, wp, ps, h, wo,
                                 g1, b1, r1, n_dec, t_dec)
        h1, h1b, cls, ga, gb = _frames_mix(alpha, rb, a, u, wp, ps, h, wo, g1, b1, r1, x_outs, seq, n_sample)
        y = _moe(h1b, cls.reshape(-1), ga.reshape(-1), gb.reshape(-1),
                 w_gate[l].astype(BF16), w_up[l].astype(BF16), w_down[l].astype(BF16))
        keep = state_pool.shape[2]
        last = (TM + (jnp.arange(bsz, dtype=I32)[:, None] + 1) * seq - keep + jnp.arange(keep, dtype=I32)[None, :])
        pools_p.append(_rows(u, last.reshape(-1)).reshape(bsz, keep, -1))
        pools_s.append(npool[:n_dec])
        ks_s.append(kx[:n_sample].reshape(n_dec, t_dec, n_heads, HEAD_W))
        vs_s.append(vx[:n_sample].reshape(n_dec, t_dec, n_heads, HEAD_W))

    y_frames, y_extra = _tail(alpha, h1, y, row(ln2_g[depth - 1]), row(ln2_b[depth - 1]))
    return (y_frames.reshape(bsz, seq, d), y_extra[:n_sample].reshape(n_dec, t_dec, d), k_out, v_out,
            jnp.stack(pools_p), jnp.stack(ks_s), jnp.stack(vs_s), jnp.stack(pools_s))
```

```python
import functools
import math

import jax
import jax.numpy as jnp
from jax import lax
from jax.experimental import pallas as pl
from jax.experimental.pallas import tpu as pltpu

F32, BF16, I32 = jnp.float32, jnp.bfloat16, jnp.int32

CHUNK = 64
N_META = 16
HEAD_DIM = 64
HEAD_W = 2 * HEAD_DIM
POOL_WINDOWS = (2, 4, 8, 16)
POOL_HALO = 16
N_EXPERTS = 16
N_GROUPS = 4
EPG = N_EXPERTS // N_GROUPS
PAIRS = ((0, 1), (0, 2), (0, 3), (1, 2), (1, 3), (2, 3))
N_CLASSES = N_GROUPS * len(PAIRS)
LN_EPS = 1e-5
NEG = -1e30
LOG2E = 1.4426950408889634
Q_SCALE = HEAD_DIM ** -0.5 * LOG2E

TM = 512
TMOE = 256
ATT_T = 256
ATT_PAD = 128
VMEM_LIMIT = 56 * 1024 * 1024


def _cparams(*sem):
    return pltpu.CompilerParams(dimension_semantics=sem, vmem_limit_bytes=VMEM_LIMIT)


def _ln(x, g, b):
    mu = jnp.mean(x, axis=-1, keepdims=True)
    xc = x - mu
    var = jnp.mean(xc * xc, axis=-1, keepdims=True)
    return xc * lax.rsqrt(var + LN_EPS) * g + b


def _dot(a, b):
    return jnp.dot(a, b, preferred_element_type=F32)


def _dot_nt(a, b):
    return lax.dot_general(a, b, (((1,), (1,)), ((), ())), preferred_element_type=F32)


def _k1_project(h, w_ref, h_ref, q_ref, u_ref, kb_ref, vb_ref, k_ref, v_ref, kx_ref, vx_ref, is_extras):
    aw = kx_ref.shape[-1]
    hb = h.astype(BF16)
    h_ref[...] = hb
    q_ref[...] = (_dot(hb, w_ref[:, 0:aw]) * Q_SCALE).astype(BF16)
    k = _dot(hb, w_ref[:, aw:2 * aw])
    v = _dot(hb, w_ref[:, 2 * aw:3 * aw])
    u_ref[...] = _dot(hb, w_ref[:, 3 * aw:])
    kb_ref[...] = k.astype(BF16)
    vb_ref[...] = v.astype(BF16)
    k_ref[...] = k.reshape(k_ref.shape)
    v_ref[...] = v.reshape(v_ref.shape)
    if is_extras:
        kx_ref[...] = k
        vx_ref[...] = v


def _k1_first_body(xe_ref, x_ref, g_ref, b_ref, w_ref, h_ref, q_ref, u_ref, kb_ref, vb_ref, k_ref, v_ref, kx_ref,
                   vx_ref):
    outs = (w_ref, h_ref, q_ref, u_ref, kb_ref, vb_ref, k_ref, v_ref, kx_ref, vx_ref)

    @pl.when(pl.program_id(0) == 0)
    def _():
        _k1_project(_ln(xe_ref[...], g_ref[...], b_ref[...]), *outs, True)

    @pl.when(pl.program_id(0) > 0)
    def _():
        _k1_project(_ln(x_ref[...], g_ref[...], b_ref[...]), *outs, False)


def _k1_next_body(alpha, h1_ref, y_ref, g_ref, b_ref, w_ref, ko_ref, vo_ref,
                  h_ref, q_ref, u_ref, kb_ref, vb_ref, k_ref, v_ref, kx_ref, vx_ref):
    del ko_ref, vo_ref
    outs = (w_ref, h_ref, q_ref, u_ref, kb_ref, vb_ref, k_ref, v_ref, kx_ref, vx_ref)
    h = _ln(alpha * h1_ref[...] + y_ref[...].astype(F32), g_ref[...], b_ref[...])

    @pl.when(pl.program_id(0) == 0)
    def _():
        _k1_project(h, *outs, True)

    @pl.when(pl.program_id(0) > 0)
    def _():
        _k1_project(h, *outs, False)


def _k1_specs(layer, n_tiles, tiles_per_b, d, aw, pw):
    def frames_tile(i):
        return jnp.maximum(i - 1, 0)

    def kv_map(i):
        t = frames_tile(i)
        return (layer, t // tiles_per_b, pl.multiple_of(N_META + (t % tiles_per_b) * TM, 16), 0, 0)

    kv_spec = pl.BlockSpec((None, None, pl.Element(TM), pl.Element(aw // HEAD_W), pl.Element(HEAD_W)), kv_map)
    row = lambda w: pl.BlockSpec((TM, w), lambda i: (i, 0))
    const = lambda r, w: pl.BlockSpec((r, w), lambda i: (0, 0))
    out_specs = [row(d), row(aw), row(pw), row(aw), row(aw), kv_spec, kv_spec, const(TM, aw), const(TM, aw)]
    return row, const, frames_tile, out_specs


def _k1_out_shapes(n, d, aw, pw, depth, b, l):
    return [
        jax.ShapeDtypeStruct((n, d), BF16),
        jax.ShapeDtypeStruct((n, aw), BF16),
        jax.ShapeDtypeStruct((n, pw), F32),
        jax.ShapeDtypeStruct((n, aw), BF16),
        jax.ShapeDtypeStruct((n, aw), BF16),
        jax.ShapeDtypeStruct((depth, b, l, aw // HEAD_W, HEAD_W), F32),
        jax.ShapeDtypeStruct((depth, b, l, aw // HEAD_W, HEAD_W), F32),
        jax.ShapeDtypeStruct((TM, aw), F32),
        jax.ShapeDtypeStruct((TM, aw), F32),
    ]


def _k1_first(x_extra, x_frames, g, b, w, depth, bsz, seq):
    nf, d = x_frames.shape
    aw = w.shape[1] // 4
    pw = w.shape[1] - 3 * aw
    n = TM + nf
    n_tiles = n // TM
    tpb = seq // TM
    row, const, frames_tile, out_specs = _k1_specs(0, n_tiles, tpb, d, aw, pw)
    in_specs = [const(TM, d), pl.BlockSpec((TM, d), lambda i: (frames_tile(i), 0)),
                const(1, d), const(1, d), const(d, w.shape[1])]
    return pl.pallas_call(
        _k1_first_body, grid=(n_tiles,), in_specs=in_specs, out_specs=out_specs,
        out_shape=_k1_out_shapes(n, d, aw, pw, depth, bsz, N_META + seq),
        compiler_params=_cparams("arbitrary"), name="k1_first",
    )(x_extra, x_frames, g, b, w)


def _k1_next(layer, alpha, h1, y, g, b, w, k_out, v_out, seq):
    n, d = h1.shape
    depth, bsz, l, n_heads, _ = k_out.shape
    aw = n_heads * HEAD_W
    pw = w.shape[1] - 3 * aw
    n_tiles = n // TM
    tpb = seq // TM
    row, const, _, out_specs = _k1_specs(layer, n_tiles, tpb, d, aw, pw)
    any_spec = pl.BlockSpec(memory_space=pl.ANY)
    in_specs = [row(d), row(d), const(1, d), const(1, d), const(d, w.shape[1]), any_spec, any_spec]
    return pl.pallas_call(
        functools.partial(_k1_next_body, alpha), grid=(n_tiles,), in_specs=in_specs, out_specs=out_specs,
        out_shape=_k1_out_shapes(n, d, aw, pw, depth, bsz, l),
        input_output_aliases={5: 5, 6: 6},
        compiler_params=_cparams("arbitrary"), name="k1_next",
    )(h1, y, g, b, w, k_out, v_out)


def _bf16_part(x):
    bits = lax.bitcast_convert_type(x, jnp.uint32) & jnp.uint32(0xFFFF0000)
    return lax.bitcast_convert_type(bits, F32)


def _split3(c):
    hi = _bf16_part(c)
    r = c - hi
    mid = _bf16_part(r)
    return hi, mid, r - mid


def _alibi_tables(slopes, seq):
    pad = ATT_PAD - N_META
    pos = jnp.concatenate([jnp.zeros((pad,), F32), jnp.arange(N_META + seq, dtype=F32)])
    live = jnp.arange(ATT_PAD + seq) >= pad
    c = jnp.where(live[None, :], (slopes * LOG2E)[:, None] * pos[None, :], NEG)
    bias = jnp.pad(jnp.stack(_split3(c), axis=-1), ((0, 0), (0, 0), (0, HEAD_W - 3))).astype(BF16)
    rr = lax.broadcasted_iota(I32, (ATT_T, ATT_T), 0)
    cc = lax.broadcasted_iota(I32, (ATT_T, ATT_T), 1)
    ahead = jnp.maximum(cc - rr, 0).astype(F32)
    diag = jnp.where(((cc // CHUNK) <= (rr // CHUNK))[None], (-2.0 * LOG2E * slopes)[:, None, None] * ahead[None], NEG)
    return bias, diag


def _attn_body(lam_ref, q_ref, kb_ref, vb_ref, kx_ref, vx_ref, bias_ref, diag_ref, sw_ref, o_ref, ka, vm, qa1, qa2):
    t = ATT_T
    lam = lam_ref[0]
    s_len = q_ref.shape[0]
    n_sub = s_len // t
    kp = ATT_PAD
    pad = kp - N_META
    lane = lax.broadcasted_iota(I32, (1, HEAD_W), 1)
    first_half = (lane < HEAD_DIM).astype(BF16)
    ones = jnp.broadcast_to((lane < 3).astype(BF16), (s_len, HEAD_W))

    ka[0:pad, 0:HEAD_W] = jnp.zeros((pad, HEAD_W), BF16)
    ka[pad:kp, 0:HEAD_W] = kx_ref[...].astype(BF16)
    ka[kp:, 0:HEAD_W] = kb_ref[...]
    ka[:, HEAD_W:] = bias_ref[...]
    vm[0:pad, :] = jnp.zeros((pad, HEAD_W), BF16)
    vm[pad:kp, :] = vx_ref[...].astype(BF16)
    qb = q_ref[...]
    qa1[:, 0:HEAD_W] = qb * first_half
    qa2[:, 0:HEAD_W] = qb * (1 - first_half)
    qa1[:, HEAD_W:] = ones
    qa2[:, HEAD_W:] = ones

    def scores(qa, sb):
        lk = kp + (sb + 1) * t
        s = _dot_nt(qa[sb * t:(sb + 1) * t, :], ka[0:lk, :])
        return jnp.concatenate([s[:, :lk - t], s[:, lk - t:] + diag_ref[...]], axis=-1)

    nxt = [scores(qa1, 0), scores(qa2, 0)]
    for sb in range(n_sub):
        q0 = sb * t
        lk = kp + (sb + 1) * t
        cur = nxt
        nxt = []
        ps, ls = [], []
        for mp, qa in enumerate((qa1, qa2)):
            if sb + 1 < n_sub:
                nxt.append(scores(qa, sb + 1))
            s = cur[mp]
            p = jnp.exp2(s - jnp.max(s, axis=-1, keepdims=True))
            ps.append(p)
            ls.append(jnp.sum(p, axis=-1, keepdims=True))
        w = (ps[0] - ps[1] * (lam * ls[0] / ls[1])).astype(BF16)
        o = (_dot(w[:, :kp], vm[...]) + _dot(w[:, kp:], vb_ref[0:lk - kp, :])) / ls[0]
        o = o * lax.rsqrt(jnp.mean(o * o, axis=-1, keepdims=True) + LN_EPS)
        o_ref[q0:q0 + t, :] = (o * sw_ref[...]).astype(BF16)


def _prompt_attention(lam, q, kb, vb, kx, vx, bias, diag, sw, bsz, seq, meta_row):
    n, aw = q.shape
    n_heads = aw // HEAD_W
    key_rows = ATT_PAD + seq
    smem = pl.BlockSpec(memory_space=pltpu.SMEM)
    frames = pl.BlockSpec((pl.Element(seq), pl.Element(HEAD_W)),
                          lambda b, h: (pl.multiple_of(TM + b * seq, TM), pl.multiple_of(h * HEAD_W, HEAD_W)))
    meta = pl.BlockSpec((N_META, HEAD_W), lambda b, h: (meta_row // N_META, h))
    return pl.pallas_call(
        _attn_body, grid=(bsz, n_heads),
        in_specs=[smem, frames, frames, frames, meta, meta,
                  pl.BlockSpec((None, key_rows, HEAD_W), lambda b, h: (h, 0, 0)),
                  pl.BlockSpec((None, ATT_T, ATT_T), lambda b, h: (h, 0, 0)),
                  pl.BlockSpec((1, HEAD_W), lambda b, h: (0, 0))],
        out_specs=frames,
        out_shape=jax.ShapeDtypeStruct((n, aw), BF16),
        scratch_shapes=[pltpu.VMEM((key_rows, 2 * HEAD_W), BF16), pltpu.VMEM((ATT_PAD, HEAD_W), BF16),
                        pltpu.VMEM((seq, 2 * HEAD_W), BF16), pltpu.VMEM((seq, 2 * HEAD_W), BF16)],
        compiler_params=_cparams("arbitrary", "arbitrary"), name="prompt_attention",
    )(lam, q, kb, vb, kx, vx, bias, diag, sw)


def _pool_mix(ext_ref, n_rows, counts, wp_ref, ps_ref):
    gw = wp_ref.shape[-1]
    outs = []
    for g, w in enumerate(POOL_WINDOWS):
        cols = slice(g * gw, (g + 1) * gw)
        tok = ext_ref[POOL_HALO:POOL_HALO + n_rows, cols]
        acc = tok
        for i in range(1, w):
            acc = acc + ext_ref[POOL_HALO - i:POOL_HALO - i + n_rows, cols]
        if counts is None:
            d = acc * (1.0 / w) - tok
        else:
            d = acc / jnp.minimum(counts, float(w)) - tok
        outs.append(_dot(d.astype(BF16), wp_ref[g]))
    return jnp.concatenate(outs, axis=-1) * ps_ref[...]


def _route(h1, r1_ref, rb_ref, cls_ref, ga_ref, gb_ref):
    hi = h1.astype(BF16)
    lo = (h1 - hi.astype(F32)).astype(BF16)
    at = _dot(hi, r1_ref[...]).T
    bt = _dot(lo, r1_ref[...]).T
    logit = at[0:N_EXPERTS] + at[N_EXPERTS:2 * N_EXPERTS] + bt[0:N_EXPERTS]
    lg = [logit[e:e + 1, :] for e in range(N_EXPERTS)]
    m = functools.reduce(jnp.maximum, lg)
    ex = [jnp.exp(x - m) for x in lg]
    z = functools.reduce(jnp.add, ex)
    prob = [x / z for x in ex]
    sel = [prob[e] + rb_ref[e] for e in range(N_EXPERTS)]

    def top2_sum(v):
        best = None
        for a, b in PAIRS:
            s = v[a] + v[b]
            best = s if best is None else jnp.maximum(best, s)
        return best

    score = [top2_sum(sel[g * EPG:(g + 1) * EPG]) for g in range(N_GROUPS)]
    gi = jnp.zeros_like(score[0], dtype=I32)
    best = score[0]
    for g in range(1, N_GROUPS):
        better = score[g] > best
        gi = jnp.where(better, g, gi)
        best = jnp.where(better, score[g], best)

    def pick(vals, j):
        out = vals[j]
        for g in range(1, N_GROUPS):
            out = jnp.where(gi == g, vals[g * EPG + j], out)
        return out

    sg = [pick(sel, j) for j in range(EPG)]
    pg = [pick(prob, j) for j in range(EPG)]

    def argmax_first(v, skip=None):
        bi = jnp.zeros_like(gi)
        bv = None
        for j in range(EPG):
            x = v[j] if skip is None else jnp.where(skip == j, -jnp.inf, v[j])
            if bv is None:
                bv = x
            else:
                better = x > bv
                bi = jnp.where(better, j, bi)
                bv = jnp.where(better, x, bv)
        return bi

    e1 = argmax_first(sg)
    e2 = argmax_first(sg, skip=e1)

    def at_idx(v, idx):
        out = v[0]
        for j in range(1, EPG):
            out = jnp.where(idx == j, v[j], out)
        return out

    g1 = at_idx(pg, e1)
    g2 = at_idx(pg, e2)
    tot = g1 + g2
    g1 = g1 / tot
    g2 = g2 / tot
    swap = e2 < e1
    ea = jnp.where(swap, e2, e1)
    eb = jnp.where(swap, e1, e2)
    pair = jnp.zeros_like(gi)
    for p, (a, b) in enumerate(PAIRS):
        pair = jnp.where((ea == a) & (eb == b), p, pair)
    cls_ref[...] = gi * len(PAIRS) + pair
    ga_ref[...] = jnp.where(swap, g2, g1)
    gb_ref[...] = jnp.where(swap, g1, g2)


def _post(alpha, mix, h, wo_ref, g_ref, b_ref, r1_ref, rb_ref, h1_ref, cls_ref, ga_ref, gb_ref):
    h1 = _ln(alpha * h.astype(F32) + _dot(mix, wo_ref[...]), g_ref[...], b_ref[...])
    h1_ref[...] = h1
    _route(h1, r1_ref, rb_ref, cls_ref, ga_ref, gb_ref)


def _post_out_shapes(n, d):
    nt = n // TM
    return [jax.ShapeDtypeStruct((n, d), F32), jax.ShapeDtypeStruct((nt, 1, TM), I32),
            jax.ShapeDtypeStruct((nt, 1, TM), F32), jax.ShapeDtypeStruct((nt, 1, TM), F32)]


def _kx_body(alpha, n_dec, t_dec, slope_ref, lam_ref, rb_ref, q_ref, kx_ref, vx_ref, ux_ref, ck_ref, cv_ref,
             sp_ref, sw_ref, wp_ref, ps_ref, h_ref, wo_ref, g_ref, b_ref, r1_ref,
             h1_ref, cls_ref, ga_ref, gb_ref, npool_ref, mix, ext):
    s = pl.program_id(0)
    past = ck_ref.shape[0]
    aw = q_ref.shape[1]
    n_heads = aw // HEAD_W
    is_meta = s == n_dec
    lam = lam_ref[0]
    r0 = pl.multiple_of(s * t_dec, t_dec)

    @pl.when(s == 0)
    def _():
        mix[...] = jnp.zeros_like(mix)

    q = q_ref[pl.ds(r0, t_dec), :].astype(F32)
    kn = kx_ref[pl.ds(r0, t_dec), :]
    vn = vx_ref[pl.ds(r0, t_dec), :]
    rq = lax.broadcasted_iota(I32, (t_dec, past), 0)
    cp = lax.broadcasted_iota(I32, (t_dec, past), 1)
    dist_p = (past + rq - cp).astype(F32)
    rn = lax.broadcasted_iota(I32, (t_dec, t_dec), 0)
    cn = lax.broadcasted_iota(I32, (t_dec, t_dec), 1)
    dist_n = jnp.abs(rn - cn).astype(F32)
    heads = []
    for hd in range(n_heads):
        slope = slope_ref[hd] * LOG2E
        c0 = hd * HEAD_W
        ckh = ck_ref[:, hd, :].astype(BF16)
        cvh = cv_ref[:, hd, :].astype(BF16)
        maps = []
        for half in range(2):
            lo_c = c0 + half * HEAD_DIM
            qh = q[:, lo_c:lo_c + HEAD_DIM].astype(BF16)
            sp = _dot_nt(qh, ckh[:, half * HEAD_DIM:(half + 1) * HEAD_DIM]) - slope * dist_p
            sp = jnp.where(is_meta, NEG, sp)
            sn = _dot_nt(qh, kn[:, lo_c:lo_c + HEAD_DIM].astype(BF16)) - slope * dist_n
            m = jnp.maximum(jnp.max(sp, axis=-1, keepdims=True), jnp.max(sn, axis=-1, keepdims=True))
            pp = jnp.exp2(sp - m)
            pn = jnp.exp2(sn - m)
            z = jnp.sum(pp, axis=-1, keepdims=True) + jnp.sum(pn, axis=-1, keepdims=True)
            maps.append((pp / z, pn / z))
        wp_ = (maps[0][0] - lam * maps[1][0]).astype(BF16)
        wn_ = (maps[0][1] - lam * maps[1][1]).astype(BF16)
        o = _dot(wp_, cvh) + _dot(wn_, vn[:, c0:c0 + HEAD_W].astype(BF16))
        o = o * lax.rsqrt(jnp.mean(o * o, axis=-1, keepdims=True) + LN_EPS)
        heads.append(o * sw_ref[...])
    attn = jnp.concatenate(heads, axis=-1)

    hist = jnp.where(is_meta, 0.0, sp_ref[...])
    u = ux_ref[pl.ds(r0, t_dec), :]
    ext[POOL_HALO - sp_ref.shape[0]:POOL_HALO, :] = hist
    ext[POOL_HALO:POOL_HALO + t_dec, :] = u
    pos = lax.broadcasted_iota(I32, (t_dec, 1), 0) + jnp.where(is_meta, 0, past)
    pool = _pool_mix(ext, t_dec, (pos + 1).astype(F32), wp_ref, ps_ref)
    mix[pl.ds(r0, t_dec), :] = jnp.concatenate([attn, pool], axis=-1).astype(BF16)
    keep = sp_ref.shape[0]
    npool_ref[...] = ext[POOL_HALO + t_dec - keep:POOL_HALO + t_dec, :]

    @pl.when(s == n_dec)
    def _():
        _post(alpha, mix[...], h_ref[...], wo_ref, g_ref, b_ref, r1_ref, rb_ref,
              h1_ref, cls_ref, ga_ref, gb_ref)


def _extras(layer, alpha, slopes, lam, rb, q, kx, vx, u, cache_k, cache_v, state_pool, sw, wp, ps, h, wo, g, b,
            r1, n_dec, t_dec):
    n, d = h.shape
    aw = q.shape[1]
    pw = u.shape[1]
    past = cache_k.shape[2]
    keep = state_pool.shape[2]
    smem = pl.BlockSpec(memory_space=pltpu.SMEM)
    tile0 = lambda w: pl.BlockSpec((TM, w), lambda s: (0, 0))
    full = lambda *shape: pl.BlockSpec(shape, lambda s: (0,) * len(shape))
    dec = lambda s: jnp.minimum(s, n_dec - 1)
    cache_spec = pl.BlockSpec((None, None, past, aw // HEAD_W, HEAD_W), lambda s: (layer, dec(s), 0, 0, 0))
    small = pl.BlockSpec((None, 1, TM), lambda s: (0, 0, 0))
    out_shapes = _post_out_shapes(n, d) + [jax.ShapeDtypeStruct((n_dec + 1, keep, pw), F32)]
    out_specs = [tile0(d), small, small, small, pl.BlockSpec((None, keep, pw), lambda s: (s, 0, 0))]
    return pl.pallas_call(
        functools.partial(_kx_body, alpha, n_dec, t_dec), grid=(n_dec + 1,),
        in_specs=[smem, smem, smem, tile0(aw), full(TM, aw), full(TM, aw), tile0(pw), cache_spec, cache_spec,
                  pl.BlockSpec((None, None, keep, pw), lambda s: (layer, dec(s), 0, 0)),
                  full(1, HEAD_W), full(*wp.shape), full(1, pw), tile0(d), full(*wo.shape), full(1, d), full(1, d),
                  full(*r1.shape)],
        out_specs=out_specs, out_shape=out_shapes,
        scratch_shapes=[pltpu.VMEM((TM, aw + pw), BF16), pltpu.VMEM((POOL_HALO + t_dec, pw), F32)],
        compiler_params=_cparams("arbitrary"), name="extras_mix",
    )(slopes, lam, rb, q, kx, vx, u, cache_k, cache_v, state_pool, sw, wp, ps, h, wo, g, b, r1)


def _k3_body(alpha, tiles_per_b, rb_ref, a_ref, u_ref, uprev_ref, umeta_ref, wp_ref, ps_ref, h_ref, wo_ref, g_ref,
             b_ref, r1_ref, h1_in, cls_in, ga_in, gb_in, h1_ref, cls_ref, ga_ref, gb_ref, ext):
    del h1_in, cls_in, ga_in, gb_in
    first = (pl.program_id(0) % tiles_per_b) == 0
    ext[0:POOL_HALO, :] = jnp.where(first, umeta_ref[...], uprev_ref[...])
    ext[POOL_HALO:, :] = u_ref[...]
    pool = _pool_mix(ext, TM, None, wp_ref, ps_ref)
    mix = jnp.concatenate([a_ref[...], pool.astype(BF16)], axis=-1)
    _post(alpha, mix, h_ref[...], wo_ref, g_ref, b_ref, r1_ref, rb_ref, h1_ref, cls_ref, ga_ref, gb_ref)


def _frames_mix(alpha, rb, a, u, wp, ps, h, wo, g, b, r1, prev_outs, seq, meta_row):
    n, d = h.shape
    aw, pw = a.shape[1], u.shape[1]
    n_tiles = n // TM - 1
    tpb = seq // TM
    smem = pl.BlockSpec(memory_space=pltpu.SMEM)
    row = lambda w: pl.BlockSpec((TM, w), lambda i: (i + 1, 0))
    full = lambda *shape: pl.BlockSpec(shape, lambda i: (0,) * len(shape))
    hb = TM // POOL_HALO
    small = pl.BlockSpec((None, 1, TM), lambda i: (i + 1, 0, 0))
    any_spec = pl.BlockSpec(memory_space=pl.ANY)
    return pl.pallas_call(
        functools.partial(_k3_body, alpha, tpb), grid=(n_tiles,),
        in_specs=[smem, row(aw), row(pw),
                  pl.BlockSpec((POOL_HALO, pw), lambda i: ((i + 1) * hb - 1, 0)),
                  pl.BlockSpec((POOL_HALO, pw), lambda i: (meta_row // POOL_HALO, 0)),
                  full(*wp.shape), full(1, pw), row(d), full(*wo.shape), full(1, d), full(1, d), full(*r1.shape)]
                 + [any_spec] * 4,
        out_specs=[row(d), small, small, small],
        out_shape=_post_out_shapes(n, d),
        input_output_aliases={12: 0, 13: 1, 14: 2, 15: 3},
        scratch_shapes=[pltpu.VMEM((POOL_HALO + TM, pw), F32)],
        compiler_params=_cparams("arbitrary"), name="frames_mix",
    )(rb, a, u, u, u, wp, ps, h, wo, g, b, r1, *prev_outs)


def _moe_body(ea_ref, eb_ref, valid_ref, fresh_ref, x_ref, ga_ref, gb_ref, wga_f, wua_f, wda_f, wgb_f, wub_f, wdb_f,
              y_ref, wga, wua, wda, wgb, wub, wdb):
    del ea_ref, eb_ref

    @pl.when(fresh_ref[pl.program_id(0)] > 0)
    def _():
        for src, dst in ((wga_f, wga), (wua_f, wua), (wda_f, wda), (wgb_f, wgb), (wub_f, wub), (wdb_f, wdb)):
            dst[...] = src[...].astype(BF16)

    @pl.when(valid_ref[pl.program_id(0)] > 0)
    def _():
        x = x_ref[...].astype(BF16)
        rows = x.shape[0]

        def lane_bcast(g_ref):
            return jnp.broadcast_to(g_ref[...], (HEAD_W, rows)).T

        def expert(wg, wu, wd, gate):
            gt = _dot(x, wg[...])
            hid = gt * (1.0 / (1.0 + jnp.exp(-gt))) * _dot(x, wu[...])
            reps = hid.shape[1] // HEAD_W
            hid = hid * jnp.concatenate([gate] * reps, axis=-1)
            return _dot(hid.astype(BF16), wd[...])

        y = expert(wga, wua, wda, lane_bcast(ga_ref))
        y_ref[...] = y + expert(wgb, wub, wdb, lane_bcast(gb_ref))

    @pl.when(valid_ref[pl.program_id(0)] == 0)
    def _():
        y_ref[...] = jnp.zeros_like(y_ref)


def _grouped_experts(layer, ea, eb, valid, fresh, xs, gas, gbs, w_gate, w_up, w_down):
    n_pad, d = xs.shape
    n_tiles = n_pad // TMOE
    de = w_gate.shape[3]
    wa = lambda r, c: pl.BlockSpec((None, None, r, c), lambda t, ea, eb, valid, fresh: (layer, ea[t], 0, 0))
    wb = lambda r, c: pl.BlockSpec((None, None, r, c), lambda t, ea, eb, valid, fresh: (layer, eb[t], 0, 0))
    gate_spec = pl.BlockSpec((None, 1, TMOE), lambda t, ea, eb, valid, fresh: (t, 0, 0))
    rows = pl.BlockSpec((TMOE, d), lambda t, ea, eb, valid, fresh: (t, 0))
    grid_spec = pltpu.PrefetchScalarGridSpec(
        num_scalar_prefetch=4, grid=(n_tiles,),
        in_specs=[rows, gate_spec, gate_spec, wa(d, de), wa(d, de), wa(de, d), wb(d, de), wb(d, de), wb(de, d)],
        out_specs=rows,
        scratch_shapes=[pltpu.VMEM((d, de), BF16), pltpu.VMEM((d, de), BF16), pltpu.VMEM((de, d), BF16)] * 2,
    )
    return pl.pallas_call(
        _moe_body, grid_spec=grid_spec, out_shape=jax.ShapeDtypeStruct((n_pad, d), F32),
        compiler_params=_cparams("arbitrary"), name="grouped_experts",
    )(ea, eb, valid, fresh, xs, gas.reshape(n_tiles, 1, TMOE), gbs.reshape(n_tiles, 1, TMOE),
      w_gate, w_up, w_down, w_gate, w_up, w_down)


def _rows(x, idx):
    return x.at[idx].get(mode="promise_in_bounds")


def _dispatch_plan(cls):
    n = cls.shape[0]
    n_pad = n + N_CLASSES * TMOE
    classes = jnp.arange(N_CLASSES, dtype=I32)
    onehot = (cls[:, None] == classes[None, :]).astype(I32)
    rank = jnp.sum(jnp.cumsum(onehot, axis=0) * onehot, axis=1) - 1
    counts = jnp.sum(onehot, axis=0)
    padded = (counts + TMOE - 1) // TMOE * TMOE
    ends = jnp.cumsum(padded)
    starts = ends - padded
    first = jnp.cumsum(counts) - counts
    pos = jnp.sum(onehot * starts[None, :], axis=1) + rank
    order = jnp.argsort(cls, stable=True).astype(I32)
    row = jnp.arange(n_pad, dtype=I32)
    row_cls = jnp.minimum(jnp.sum((ends[None, :] <= row[:, None]).astype(I32), axis=1), N_CLASSES - 1)
    row_hot = (row_cls[:, None] == classes[None, :]).astype(I32)
    within = row - jnp.sum(row_hot * starts[None, :], axis=1)
    live = within < jnp.sum(row_hot * counts[None, :], axis=1)
    src = _rows(order, jnp.clip(jnp.sum(row_hot * first[None, :], axis=1) + within, 0, n - 1))
    src = jnp.where(live, src, 0)
    tile_cls = row_cls[::TMOE]
    valid = (row[::TMOE] < ends[-1]).astype(I32)
    pa = jnp.array([p[0] for p in PAIRS], I32)
    pb = jnp.array([p[1] for p in PAIRS], I32)
    grp = tile_cls // len(PAIRS)
    pair_hot = ((tile_cls % len(PAIRS))[:, None] == jnp.arange(len(PAIRS), dtype=I32)[None, :]).astype(I32)
    ea = grp * EPG + jnp.sum(pair_hot * pa[None, :], axis=1)
    eb = grp * EPG + jnp.sum(pair_hot * pb[None, :], axis=1)
    fresh = jnp.concatenate([jnp.ones((1,), I32), (tile_cls[1:] != tile_cls[:-1]).astype(I32)])
    return pos.astype(I32), src, live, ea.astype(I32), eb.astype(I32), valid, fresh


def _moe(layer, h1, cls, ga, gb, w_gate, w_up, w_down):
    pos, src, live, ea, eb, valid, fresh = _dispatch_plan(cls)
    xs = _rows(h1, src)
    gas = jnp.where(live, _rows(ga, src), 0.0)
    gbs = jnp.where(live, _rows(gb, src), 0.0)
    ys = _grouped_experts(layer, ea, eb, valid, fresh, xs, gas, gbs, w_gate, w_up, w_down)
    return _rows(ys, pos)


def _tail_body(alpha, h1_ref, y_ref, g_ref, b_ref, of_ref, ox_ref):
    out = _ln(alpha * h1_ref[...] + y_ref[...].astype(F32), g_ref[...], b_ref[...])
    of_ref[...] = out

    @pl.when(pl.program_id(0) == 0)
    def _():
        ox_ref[...] = out


def _tail(alpha, h1, y, g, b):
    n, d = h1.shape
    row = pl.BlockSpec((TM, d), lambda i: (i, 0))
    const = pl.BlockSpec((1, d), lambda i: (0, 0))
    return pl.pallas_call(
        functools.partial(_tail_body, alpha), grid=(n // TM,),
        in_specs=[row, row, const, const],
        out_specs=[pl.BlockSpec((TM, d), lambda i: (jnp.maximum(i - 1, 0), 0)),
                   pl.BlockSpec((TM, d), lambda i: (0, 0))],
        out_shape=[jax.ShapeDtypeStruct((n - TM, d), F32), jax.ShapeDtypeStruct((TM, d), F32)],
        compiler_params=_cparams("arbitrary"), name="final_norm",
    )(h1, y, g, b)


def kernel(x_prompt, x_sample, cache_k, cache_v, state_pool, meta_tokens, ln_in_g, ln_in_b, w_in, lambda_qk,
           subln_w, w_pool, pool_scale, w_out, ln1_g, ln1_b, router_w, router_b, w_gate, w_up, w_down,
           ln2_g, ln2_b):
    bsz, seq, d = x_prompt.shape
    n_dec, t_dec, _ = x_sample.shape
    depth = w_in.shape[0]
    past = cache_k.shape[2]
    n_heads = cache_k.shape[3]
    aw = n_heads * HEAD_W
    alpha = (2 * depth) ** 0.25
    n_sample = n_dec * t_dec
    assert seq % TM == 0 and n_sample + N_META <= TM and t_dec == N_META and n_sample % POOL_HALO == 0
    assert past % CHUNK == 0 and t_dec <= CHUNK

    x_extra = jnp.concatenate([x_sample.reshape(n_sample, d), meta_tokens.astype(F32),
                               jnp.zeros((TM - n_sample - N_META, d), F32)], axis=0)
    x_frames = x_prompt.reshape(bsz * seq, d)
    row = lambda v: v.reshape(1, -1).astype(F32)
    slopes = (2.0 ** (-8.0 * jnp.arange(1, n_heads + 1, dtype=F32) / n_heads)).astype(F32)
    bias, diag = _alibi_tables(slopes, seq)
    rw_hi = _bf16_part(router_w.astype(F32))
    rw_lo = router_w.astype(F32) - rw_hi
    r1 = jnp.concatenate([rw_hi, rw_lo, jnp.zeros((d, HEAD_W - 2 * N_EXPERTS), F32)], axis=1).astype(BF16)
    rb = router_b.astype(F32)

    k_out = v_out = None
    h1 = y = None
    pools_p, pools_s, ks_s, vs_s = [], [], [], []
    for l in range(depth):
        lam_init = 0.8 - 0.6 * math.exp(-0.3 * l)
        lq = lambda_qk[l].astype(F32)
        lam = (jnp.exp(jnp.sum(lq[0] * lq[1])) - jnp.exp(jnp.sum(lq[2] * lq[3])) + lam_init).reshape(1)
        sw = row(subln_w[l]) * (1.0 - lam_init)
        w_in_l = w_in[l].astype(BF16)
        if l == 0:
            k1_outs = _k1_first(x_extra, x_frames, row(ln_in_g), row(ln_in_b), w_in_l, depth, bsz, seq)
        else:
            k1_outs = _k1_next(l, alpha, h1, y, row(ln2_g[l - 1]), row(ln2_b[l - 1]), w_in_l, k_out, v_out, seq)
        h, q, u, kb, vb, k_out, v_out, kx, vx = k1_outs
        meta_shape = (bsz, N_META, n_heads, HEAD_W)
        meta_k = jnp.broadcast_to(kx[n_sample:n_sample + N_META].reshape(1, *meta_shape[1:]), meta_shape)
        meta_v = jnp.broadcast_to(vx[n_sample:n_sample + N_META].reshape(1, *meta_shape[1:]), meta_shape)
        k_out = k_out.at[l, :, :N_META].set(meta_k)
        v_out = v_out.at[l, :, :N_META].set(meta_v)
        a = _prompt_attention(lam, q, kb, vb, kx, vx, bias, diag, sw, bsz, seq, n_sample)
        wp = w_pool[l].astype(BF16)
        ps = row(pool_scale[l])
        wo = w_out[l].astype(BF16)
        g1, b1 = row(ln1_g[l]), row(ln1_b[l])
        *x_outs, npool = _extras(l, alpha, slopes, lam, rb, q, kx, vx, u, cache_k, cache_v, state_pool, sw, wp, ps, h, wo,
                                 g1, b1, r1, n_dec, t_dec)
        h1, cls, ga, gb = _frames_mix(alpha, rb, a, u, wp, ps, h, wo, g1, b1, r1, x_outs, seq, n_sample)
        y = _moe(l, h1, cls.reshape(-1), ga.reshape(-1), gb.reshape(-1), w_gate, w_up, w_down)
        keep = state_pool.shape[2]
        last = (TM + (jnp.arange(bsz, dtype=I32)[:, None] + 1) * seq - keep + jnp.arange(keep, dtype=I32)[None, :])
        pools_p.append(_rows(u, last.reshape(-1)).reshape(bsz, keep, -1))
        pools_s.append(npool[:n_dec])
        ks_s.append(kx[:n_sample].reshape(n_dec, t_dec, n_heads, HEAD_W))
        vs_s.append(vx[:n_sample].reshape(n_dec, t_dec, n_heads, HEAD_W))

    y_frames, y_extra = _tail(alpha, h1, y, row(ln2_g[depth - 1]), row(ln2_b[depth - 1]))
    return (y_frames.reshape(bsz, seq, d), y_extra[:n_sample].reshape(n_dec, t_dec, d), k_out, v_out,
            jnp.stack(pools_p), jnp.stack(ks_s), jnp.stack(vs_s), jnp.stack(pools_s))
```

```python
import functools
import math

import jax
import jax.numpy as jnp
from jax import lax
from jax.experimental import pallas as pl
from jax.experimental.pallas import tpu as pltpu

F32, BF16, I32 = jnp.float32, jnp.bfloat16, jnp.int32

CHUNK = 64
N_META = 16
HEAD_DIM = 64
HEAD_W = 2 * HEAD_DIM
POOL_WINDOWS = (2, 4, 8, 16)
POOL_HALO = 16
N_EXPERTS = 16
N_GROUPS = 4
EPG = N_EXPERTS // N_GROUPS
PAIRS = ((0, 1), (0, 2), (0, 3), (1, 2), (1, 3), (2, 3))
N_CLASSES = N_GROUPS * len(PAIRS)
LN_EPS = 1e-5
NEG = -1e30
LOG2E = 1.4426950408889634
Q_SCALE = HEAD_DIM ** -0.5 * LOG2E

TM = 512
TMOE = 256
ATT_T = 256
ATT_PAD = 128
VMEM_LIMIT = 56 * 1024 * 1024


def _cparams(*sem):
    return pltpu.CompilerParams(dimension_semantics=sem, vmem_limit_bytes=VMEM_LIMIT)


def _ln(x, g, b):
    mu = jnp.mean(x, axis=-1, keepdims=True)
    xc = x - mu
    var = jnp.mean(xc * xc, axis=-1, keepdims=True)
    return xc * lax.rsqrt(var + LN_EPS) * g + b


def _dot(a, b):
    return jnp.dot(a, b, preferred_element_type=F32)


def _dot_nt(a, b):
    return lax.dot_general(a, b, (((1,), (1,)), ((), ())), preferred_element_type=F32)


def _k1_project(h, w_ref, h_ref, q_ref, u_ref, kb_ref, vb_ref, k_ref, v_ref, kx_ref, vx_ref, is_extras):
    aw = kx_ref.shape[-1]
    hb = h.astype(BF16)
    h_ref[...] = hb
    q_ref[...] = (_dot(hb, w_ref[:, 0:aw]) * Q_SCALE).astype(BF16)
    k = _dot(hb, w_ref[:, aw:2 * aw])
    v = _dot(hb, w_ref[:, 2 * aw:3 * aw])
    u_ref[...] = _dot(hb, w_ref[:, 3 * aw:])
    kb_ref[...] = k.astype(BF16)
    vb_ref[...] = v.astype(BF16)
    k_ref[...] = k.reshape(k_ref.shape)
    v_ref[...] = v.reshape(v_ref.shape)
    if is_extras:
        kx_ref[...] = k
        vx_ref[...] = v


def _k1_first_body(xe_ref, x_ref, g_ref, b_ref, w_ref, h_ref, q_ref, u_ref, kb_ref, vb_ref, k_ref, v_ref, kx_ref,
                   vx_ref):
    outs = (w_ref, h_ref, q_ref, u_ref, kb_ref, vb_ref, k_ref, v_ref, kx_ref, vx_ref)

    @pl.when(pl.program_id(0) == 0)
    def _():
        _k1_project(_ln(xe_ref[...], g_ref[...], b_ref[...]), *outs, True)

    @pl.when(pl.program_id(0) > 0)
    def _():
        _k1_project(_ln(x_ref[...], g_ref[...], b_ref[...]), *outs, False)


def _k1_next_body(alpha, h1_ref, y_ref, g_ref, b_ref, w_ref, ko_ref, vo_ref,
                  h_ref, q_ref, u_ref, kb_ref, vb_ref, k_ref, v_ref, kx_ref, vx_ref):
    del ko_ref, vo_ref
    outs = (w_ref, h_ref, q_ref, u_ref, kb_ref, vb_ref, k_ref, v_ref, kx_ref, vx_ref)
    h = _ln(alpha * h1_ref[...] + y_ref[...].astype(F32), g_ref[...], b_ref[...])

    @pl.when(pl.program_id(0) == 0)
    def _():
        _k1_project(h, *outs, True)

    @pl.when(pl.program_id(0) > 0)
    def _():
        _k1_project(h, *outs, False)


def _k1_specs(layer, n_tiles, tiles_per_b, d, aw, pw):
    def frames_tile(i):
        return jnp.maximum(i - 1, 0)

    def kv_map(i):
        t = frames_tile(i)
        return (layer, t // tiles_per_b, pl.multiple_of(N_META + (t % tiles_per_b) * TM, 16), 0, 0)

    kv_spec = pl.BlockSpec((None, None, pl.Element(TM), pl.Element(aw // HEAD_W), pl.Element(HEAD_W)), kv_map)
    row = lambda w: pl.BlockSpec((TM, w), lambda i: (i, 0))
    const = lambda r, w: pl.BlockSpec((r, w), lambda i: (0, 0))
    out_specs = [row(d), row(aw), row(pw), row(aw), row(aw), kv_spec, kv_spec, const(TM, aw), const(TM, aw)]
    return row, const, frames_tile, out_specs


def _k1_out_shapes(n, d, aw, pw, depth, b, l):
    return [
        jax.ShapeDtypeStruct((n, d), BF16),
        jax.ShapeDtypeStruct((n, aw), BF16),
        jax.ShapeDtypeStruct((n, pw), F32),
        jax.ShapeDtypeStruct((n, aw), BF16),
        jax.ShapeDtypeStruct((n, aw), BF16),
        jax.ShapeDtypeStruct((depth, b, l, aw // HEAD_W, HEAD_W), F32),
        jax.ShapeDtypeStruct((depth, b, l, aw // HEAD_W, HEAD_W), F32),
        jax.ShapeDtypeStruct((TM, aw), F32),
        jax.ShapeDtypeStruct((TM, aw), F32),
    ]


def _k1_first(x_extra, x_frames, g, b, w, depth, bsz, seq):
    nf, d = x_frames.shape
    aw = w.shape[1] // 4
    pw = w.shape[1] - 3 * aw
    n = TM + nf
    n_tiles = n // TM
    tpb = seq // TM
    row, const, frames_tile, out_specs = _k1_specs(0, n_tiles, tpb, d, aw, pw)
    in_specs = [const(TM, d), pl.BlockSpec((TM, d), lambda i: (frames_tile(i), 0)),
                const(1, d), const(1, d), const(d, w.shape[1])]
    return pl.pallas_call(
        _k1_first_body, grid=(n_tiles,), in_specs=in_specs, out_specs=out_specs,
        out_shape=_k1_out_shapes(n, d, aw, pw, depth, bsz, N_META + seq),
        compiler_params=_cparams("arbitrary"), name="k1_first",
    )(x_extra, x_frames, g, b, w)


def _k1_next(layer, alpha, h1, y, g, b, w, k_out, v_out, seq):
    n, d = h1.shape
    depth, bsz, l, n_heads, _ = k_out.shape
    aw = n_heads * HEAD_W
    pw = w.shape[1] - 3 * aw
    n_tiles = n // TM
    tpb = seq // TM
    row, const, _, out_specs = _k1_specs(layer, n_tiles, tpb, d, aw, pw)
    any_spec = pl.BlockSpec(memory_space=pl.ANY)
    in_specs = [row(d), row(d), const(1, d), const(1, d), const(d, w.shape[1]), any_spec, any_spec]
    return pl.pallas_call(
        functools.partial(_k1_next_body, alpha), grid=(n_tiles,), in_specs=in_specs, out_specs=out_specs,
        out_shape=_k1_out_shapes(n, d, aw, pw, depth, bsz, l),
        input_output_aliases={5: 5, 6: 6},
        compiler_params=_cparams("arbitrary"), name="k1_next",
    )(h1, y, g, b, w, k_out, v_out)


def _bf16_part(x):
    bits = lax.bitcast_convert_type(x, jnp.uint32) & jnp.uint32(0xFFFF0000)
    return lax.bitcast_convert_type(bits, F32)


def _split3(c):
    hi = _bf16_part(c)
    r = c - hi
    mid = _bf16_part(r)
    return hi, mid, r - mid


def _alibi_tables(slopes, seq):
    pad = ATT_PAD - N_META
    pos = jnp.concatenate([jnp.zeros((pad,), F32), jnp.arange(N_META + seq, dtype=F32)])
    live = jnp.arange(ATT_PAD + seq) >= pad
    c = jnp.where(live[None, :], (slopes * LOG2E)[:, None] * pos[None, :], NEG)
    bias = jnp.pad(jnp.stack(_split3(c), axis=-1), ((0, 0), (0, 0), (0, HEAD_W - 3))).astype(BF16)
    rr = lax.broadcasted_iota(I32, (ATT_T, ATT_T), 0)
    cc = lax.broadcasted_iota(I32, (ATT_T, ATT_T), 1)
    ahead = jnp.maximum(cc - rr, 0).astype(F32)
    diag = jnp.where(((cc // CHUNK) <= (rr // CHUNK))[None], (-2.0 * LOG2E * slopes)[:, None, None] * ahead[None], NEG)
    return bias, diag


def _attn_body(lam_ref, q_ref, kb_ref, vb_ref, kx_ref, vx_ref, bias_ref, diag_ref, sw_ref, o_ref, ka, vm, qa1, qa2):
    t = ATT_T
    lam = lam_ref[0]
    s_len = q_ref.shape[0]
    n_sub = s_len // t
    kp = ATT_PAD
    pad = kp - N_META
    lane = lax.broadcasted_iota(I32, (1, HEAD_W), 1)
    first_half = (lane < HEAD_DIM).astype(BF16)
    ones = jnp.broadcast_to((lane < 3).astype(BF16), (s_len, HEAD_W))

    ka[0:pad, 0:HEAD_W] = jnp.zeros((pad, HEAD_W), BF16)
    ka[pad:kp, 0:HEAD_W] = kx_ref[...].astype(BF16)
    ka[kp:, 0:HEAD_W] = kb_ref[...]
    ka[:, HEAD_W:] = bias_ref[...]
    vm[0:pad, :] = jnp.zeros((pad, HEAD_W), BF16)
    vm[pad:kp, :] = vx_ref[...].astype(BF16)
    qb = q_ref[...]
    qa1[:, 0:HEAD_W] = qb * first_half
    qa2[:, 0:HEAD_W] = qb * (1 - first_half)
    qa1[:, HEAD_W:] = ones
    qa2[:, HEAD_W:] = ones

    def scores(qa, sb):
        lk = kp + (sb + 1) * t
        s = _dot_nt(qa[sb * t:(sb + 1) * t, :], ka[0:lk, :])
        return jnp.concatenate([s[:, :lk - t], s[:, lk - t:] + diag_ref[...]], axis=-1)

    nxt = [scores(qa1, 0), scores(qa2, 0)]
    for sb in range(n_sub):
        q0 = sb * t
        lk = kp + (sb + 1) * t
        cur = nxt
        nxt = []
        ps, ls = [], []
        for mp, qa in enumerate((qa1, qa2)):
            if sb + 1 < n_sub:
                nxt.append(scores(qa, sb + 1))
            s = cur[mp]
            p = jnp.exp2(s - jnp.max(s, axis=-1, keepdims=True))
            ps.append(p)
            ls.append(jnp.sum(p, axis=-1, keepdims=True))
        w = (ps[0] - ps[1] * (lam * ls[0] / ls[1])).astype(BF16)
        o = (_dot(w[:, :kp], vm[...]) + _dot(w[:, kp:], vb_ref[0:lk - kp, :])) / ls[0]
        o = o * lax.rsqrt(jnp.mean(o * o, axis=-1, keepdims=True) + LN_EPS)
        o_ref[q0:q0 + t, :] = (o * sw_ref[...]).astype(BF16)


def _prompt_attention(lam, q, kb, vb, kx, vx, bias, diag, sw, bsz, seq, meta_row):
    n, aw = q.shape
    n_heads = aw // HEAD_W
    key_rows = ATT_PAD + seq
    smem = pl.BlockSpec(memory_space=pltpu.SMEM)
    frames = pl.BlockSpec((pl.Element(seq), pl.Element(HEAD_W)),
                          lambda b, h: (pl.multiple_of(TM + b * seq, TM), pl.multiple_of(h * HEAD_W, HEAD_W)))
    meta = pl.BlockSpec((N_META, HEAD_W), lambda b, h: (meta_row // N_META, h))
    return pl.pallas_call(
        _attn_body, grid=(bsz, n_heads),
        in_specs=[smem, frames, frames, frames, meta, meta,
                  pl.BlockSpec((None, key_rows, HEAD_W), lambda b, h: (h, 0, 0)),
                  pl.BlockSpec((None, ATT_T, ATT_T), lambda b, h: (h, 0, 0)),
                  pl.BlockSpec((1, HEAD_W), lambda b, h: (0, 0))],
        out_specs=frames,
        out_shape=jax.ShapeDtypeStruct((n, aw), BF16),
        scratch_shapes=[pltpu.VMEM((key_rows, 2 * HEAD_W), BF16), pltpu.VMEM((ATT_PAD, HEAD_W), BF16),
                        pltpu.VMEM((seq, 2 * HEAD_W), BF16), pltpu.VMEM((seq, 2 * HEAD_W), BF16)],
        compiler_params=_cparams("arbitrary", "arbitrary"), name="prompt_attention",
    )(lam, q, kb, vb, kx, vx, bias, diag, sw)


def _pool_mix(ext_ref, n_rows, counts, wp_ref, ps_ref, row0=0):
    gw = wp_ref.shape[-1]
    outs = []
    base = POOL_HALO + row0
    for g, w in enumerate(POOL_WINDOWS):
        cols = slice(g * gw, (g + 1) * gw)
        tok = ext_ref[base:base + n_rows, cols]
        acc = tok
        for i in range(1, w):
            acc = acc + ext_ref[base - i:base - i + n_rows, cols]
        if counts is None:
            d = acc * (1.0 / w) - tok
        else:
            d = acc / jnp.minimum(counts, float(w)) - tok
        outs.append(_dot(d.astype(BF16), wp_ref[g]))
    return jnp.concatenate(outs, axis=-1) * ps_ref[...]


def _route(h1, r1_ref, rb_ref, cls_ref, ga_ref, gb_ref):
    hi = h1.astype(BF16)
    lo = (h1 - hi.astype(F32)).astype(BF16)
    at = _dot(hi, r1_ref[...]).T
    bt = _dot(lo, r1_ref[...]).T
    logit = at[0:N_EXPERTS] + at[N_EXPERTS:2 * N_EXPERTS] + bt[0:N_EXPERTS]
    lg = [logit[e:e + 1, :] for e in range(N_EXPERTS)]
    m = functools.reduce(jnp.maximum, lg)
    ex = [jnp.exp(x - m) for x in lg]
    z = functools.reduce(jnp.add, ex)
    prob = [x / z for x in ex]
    sel = [prob[e] + rb_ref[e] for e in range(N_EXPERTS)]

    def top2_sum(v):
        best = None
        for a, b in PAIRS:
            s = v[a] + v[b]
            best = s if best is None else jnp.maximum(best, s)
        return best

    score = [top2_sum(sel[g * EPG:(g + 1) * EPG]) for g in range(N_GROUPS)]
    gi = jnp.zeros_like(score[0], dtype=I32)
    best = score[0]
    for g in range(1, N_GROUPS):
        better = score[g] > best
        gi = jnp.where(better, g, gi)
        best = jnp.where(better, score[g], best)

    def pick(vals, j):
        out = vals[j]
        for g in range(1, N_GROUPS):
            out = jnp.where(gi == g, vals[g * EPG + j], out)
        return out

    sg = [pick(sel, j) for j in range(EPG)]
    pg = [pick(prob, j) for j in range(EPG)]

    def argmax_first(v, skip=None):
        bi = jnp.zeros_like(gi)
        bv = None
        for j in range(EPG):
            x = v[j] if skip is None else jnp.where(skip == j, -jnp.inf, v[j])
            if bv is None:
                bv = x
            else:
                better = x > bv
                bi = jnp.where(better, j, bi)
                bv = jnp.where(better, x, bv)
        return bi

    e1 = argmax_first(sg)
    e2 = argmax_first(sg, skip=e1)

    def at_idx(v, idx):
        out = v[0]
        for j in range(1, EPG):
            out = jnp.where(idx == j, v[j], out)
        return out

    g1 = at_idx(pg, e1)
    g2 = at_idx(pg, e2)
    tot = g1 + g2
    g1 = g1 / tot
    g2 = g2 / tot
    swap = e2 < e1
    ea = jnp.where(swap, e2, e1)
    eb = jnp.where(swap, e1, e2)
    pair = jnp.zeros_like(gi)
    for p, (a, b) in enumerate(PAIRS):
        pair = jnp.where((ea == a) & (eb == b), p, pair)
    cls_ref[...] = gi * len(PAIRS) + pair
    ga_ref[...] = jnp.where(swap, g2, g1)
    gb_ref[...] = jnp.where(swap, g1, g2)


def _post(alpha, mix, h, wo_ref, g_ref, b_ref, r1_ref, rb_ref, h1_ref, cls_ref, ga_ref, gb_ref):
    h1 = _ln(alpha * h.astype(F32) + _dot(mix, wo_ref[...]), g_ref[...], b_ref[...])
    h1_ref[...] = h1
    _route(h1, r1_ref, rb_ref, cls_ref, ga_ref, gb_ref)


def _post_out_shapes(n, d):
    nt = n // TM
    return [jax.ShapeDtypeStruct((n, d), F32), jax.ShapeDtypeStruct((nt, 1, TM), I32),
            jax.ShapeDtypeStruct((nt, 1, TM), F32), jax.ShapeDtypeStruct((nt, 1, TM), F32)]


def _kx_body(alpha, n_dec, t_dec, slope_ref, lam_ref, rb_ref, q_ref, kx_ref, vx_ref, ux_ref, ck_ref, cv_ref,
             sp_ref, sw_ref, wp_ref, ps_ref, h_ref, wo_ref, g_ref, b_ref, r1_ref,
             h1_ref, cls_ref, ga_ref, gb_ref, npool_ref, mix, ext):
    s = pl.program_id(0)
    past = ck_ref.shape[0]
    aw = q_ref.shape[1]
    n_heads = aw // HEAD_W
    is_meta = s == n_dec
    lam = lam_ref[0]
    r0 = pl.multiple_of(s * t_dec, t_dec)

    @pl.when(s == 0)
    def _():
        mix[...] = jnp.zeros_like(mix)

    q = q_ref[pl.ds(r0, t_dec), :].astype(F32)
    kn = kx_ref[pl.ds(r0, t_dec), :]
    vn = vx_ref[pl.ds(r0, t_dec), :]
    rq = lax.broadcasted_iota(I32, (t_dec, past), 0)
    cp = lax.broadcasted_iota(I32, (t_dec, past), 1)
    dist_p = (past + rq - cp).astype(F32)
    rn = lax.broadcasted_iota(I32, (t_dec, t_dec), 0)
    cn = lax.broadcasted_iota(I32, (t_dec, t_dec), 1)
    dist_n = jnp.abs(rn - cn).astype(F32)
    heads = []
    for hd in range(n_heads):
        slope = slope_ref[hd] * LOG2E
        c0 = hd * HEAD_W
        ckh = ck_ref[:, hd, :].astype(BF16)
        cvh = cv_ref[:, hd, :].astype(BF16)
        maps = []
        for half in range(2):
            lo_c = c0 + half * HEAD_DIM
            qh = q[:, lo_c:lo_c + HEAD_DIM].astype(BF16)
            sp = _dot_nt(qh, ckh[:, half * HEAD_DIM:(half + 1) * HEAD_DIM]) - slope * dist_p
            sp = jnp.where(is_meta, NEG, sp)
            sn = _dot_nt(qh, kn[:, lo_c:lo_c + HEAD_DIM].astype(BF16)) - slope * dist_n
            m = jnp.maximum(jnp.max(sp, axis=-1, keepdims=True), jnp.max(sn, axis=-1, keepdims=True))
            pp = jnp.exp2(sp - m)
            pn = jnp.exp2(sn - m)
            z = jnp.sum(pp, axis=-1, keepdims=True) + jnp.sum(pn, axis=-1, keepdims=True)
            maps.append((pp / z, pn / z))
        wp_ = (maps[0][0] - lam * maps[1][0]).astype(BF16)
        wn_ = (maps[0][1] - lam * maps[1][1]).astype(BF16)
        o = _dot(wp_, cvh) + _dot(wn_, vn[:, c0:c0 + HEAD_W].astype(BF16))
        o = o * lax.rsqrt(jnp.mean(o * o, axis=-1, keepdims=True) + LN_EPS)
        heads.append(o * sw_ref[...])
    attn = jnp.concatenate(heads, axis=-1)

    hist = jnp.where(is_meta, 0.0, sp_ref[...])
    u = ux_ref[pl.ds(r0, t_dec), :]
    ext[POOL_HALO - sp_ref.shape[0]:POOL_HALO, :] = hist
    ext[POOL_HALO:POOL_HALO + t_dec, :] = u
    pos = lax.broadcasted_iota(I32, (t_dec, 1), 0) + jnp.where(is_meta, 0, past)
    pool = _pool_mix(ext, t_dec, (pos + 1).astype(F32), wp_ref, ps_ref)
    mix[pl.ds(r0, t_dec), :] = jnp.concatenate([attn, pool], axis=-1).astype(BF16)
    keep = sp_ref.shape[0]
    npool_ref[...] = ext[POOL_HALO + t_dec - keep:POOL_HALO + t_dec, :]

    @pl.when(s == n_dec)
    def _():
        _post(alpha, mix[...], h_ref[...], wo_ref, g_ref, b_ref, r1_ref, rb_ref,
              h1_ref, cls_ref, ga_ref, gb_ref)


def _extras(layer, alpha, slopes, lam, rb, q, kx, vx, u, cache_k, cache_v, state_pool, sw, wp, ps, h, wo, g, b,
            r1, n_dec, t_dec):
    n, d = h.shape
    aw = q.shape[1]
    pw = u.shape[1]
    past = cache_k.shape[2]
    keep = state_pool.shape[2]
    smem = pl.BlockSpec(memory_space=pltpu.SMEM)
    tile0 = lambda w: pl.BlockSpec((TM, w), lambda s: (0, 0))
    full = lambda *shape: pl.BlockSpec(shape, lambda s: (0,) * len(shape))
    dec = lambda s: jnp.minimum(s, n_dec - 1)
    cache_spec = pl.BlockSpec((None, None, past, aw // HEAD_W, HEAD_W), lambda s: (layer, dec(s), 0, 0, 0))
    small = pl.BlockSpec((None, 1, TM), lambda s: (0, 0, 0))
    out_shapes = _post_out_shapes(n, d) + [jax.ShapeDtypeStruct((n_dec + 1, keep, pw), F32)]
    out_specs = [tile0(d), small, small, small, pl.BlockSpec((None, keep, pw), lambda s: (s, 0, 0))]
    return pl.pallas_call(
        functools.partial(_kx_body, alpha, n_dec, t_dec), grid=(n_dec + 1,),
        in_specs=[smem, smem, smem, tile0(aw), full(TM, aw), full(TM, aw), tile0(pw), cache_spec, cache_spec,
                  pl.BlockSpec((None, None, keep, pw), lambda s: (layer, dec(s), 0, 0)),
                  full(1, HEAD_W), full(*wp.shape), full(1, pw), tile0(d), full(*wo.shape), full(1, d), full(1, d),
                  full(*r1.shape)],
        out_specs=out_specs, out_shape=out_shapes,
        scratch_shapes=[pltpu.VMEM((TM, aw + pw), BF16), pltpu.VMEM((POOL_HALO + t_dec, pw), F32)],
        compiler_params=_cparams("arbitrary"), name="extras_mix",
    )(slopes, lam, rb, q, kx, vx, u, cache_k, cache_v, state_pool, sw, wp, ps, h, wo, g, b, r1)


def _k3_body(alpha, tiles_per_b, rb_ref, a_ref, u_ref, uprev_ref, umeta_ref, wp_ref, ps_ref, h_ref, wo_ref, g_ref,
             b_ref, r1_ref, h1_in, cls_in, ga_in, gb_in, h1_ref, cls_ref, ga_ref, gb_ref, ext):
    del h1_in, cls_in, ga_in, gb_in
    first = (pl.program_id(0) % tiles_per_b) == 0
    ext[0:POOL_HALO, :] = jnp.where(first, umeta_ref[...], uprev_ref[...])
    ext[POOL_HALO:, :] = u_ref[...]
    half = TM // 2
    pre = []
    for r0 in (0, half):
        rows = slice(r0, r0 + half)
        pool = _pool_mix(ext, half, None, wp_ref, ps_ref, row0=r0)
        mix = jnp.concatenate([a_ref[rows, :], pool.astype(BF16)], axis=-1)
        pre.append(alpha * h_ref[rows, :].astype(F32) + _dot(mix, wo_ref[...]))
    for r0, z in zip((0, half), pre):
        rows = slice(r0, r0 + half)
        h1 = _ln(z, g_ref[...], b_ref[...])
        h1_ref[rows, :] = h1
        _route(h1, r1_ref, rb_ref, cls_ref.at[:, rows], ga_ref.at[:, rows], gb_ref.at[:, rows])


def _frames_mix(alpha, rb, a, u, wp, ps, h, wo, g, b, r1, prev_outs, seq, meta_row):
    n, d = h.shape
    aw, pw = a.shape[1], u.shape[1]
    n_tiles = n // TM - 1
    tpb = seq // TM
    smem = pl.BlockSpec(memory_space=pltpu.SMEM)
    row = lambda w: pl.BlockSpec((TM, w), lambda i: (i + 1, 0))
    full = lambda *shape: pl.BlockSpec(shape, lambda i: (0,) * len(shape))
    hb = TM // POOL_HALO
    small = pl.BlockSpec((None, 1, TM), lambda i: (i + 1, 0, 0))
    any_spec = pl.BlockSpec(memory_space=pl.ANY)
    return pl.pallas_call(
        functools.partial(_k3_body, alpha, tpb), grid=(n_tiles,),
        in_specs=[smem, row(aw), row(pw),
                  pl.BlockSpec((POOL_HALO, pw), lambda i: ((i + 1) * hb - 1, 0)),
                  pl.BlockSpec((POOL_HALO, pw), lambda i: (meta_row // POOL_HALO, 0)),
                  full(*wp.shape), full(1, pw), row(d), full(*wo.shape), full(1, d), full(1, d), full(*r1.shape)]
                 + [any_spec] * 4,
        out_specs=[row(d), small, small, small],
        out_shape=_post_out_shapes(n, d),
        input_output_aliases={12: 0, 13: 1, 14: 2, 15: 3},
        scratch_shapes=[pltpu.VMEM((POOL_HALO + TM, pw), F32)],
        compiler_params=_cparams("arbitrary"), name="frames_mix",
    )(rb, a, u, u, u, wp, ps, h, wo, g, b, r1, *prev_outs)


def _moe_body(ea_ref, eb_ref, valid_ref, fresh_ref, x_ref, ga_ref, gb_ref, wga_f, wua_f, wda_f, wgb_f, wub_f, wdb_f,
              y_ref, wga, wua, wda, wgb, wub, wdb):
    del ea_ref, eb_ref

    @pl.when(fresh_ref[pl.program_id(0)] > 0)
    def _():
        for src, dst in ((wga_f, wga), (wua_f, wua), (wda_f, wda), (wgb_f, wgb), (wub_f, wub), (wdb_f, wdb)):
            dst[...] = src[...].astype(BF16)

    @pl.when(valid_ref[pl.program_id(0)] > 0)
    def _():
        x = x_ref[...].astype(BF16)
        rows = x.shape[0]

        def lane_bcast(g_ref):
            return jnp.broadcast_to(g_ref[...], (HEAD_W, rows)).T

        def expert(wg, wu, wd, gate):
            gt = _dot(x, wg[...])
            hid = gt * (1.0 / (1.0 + jnp.exp(-gt))) * _dot(x, wu[...])
            reps = hid.shape[1] // HEAD_W
            hid = hid * jnp.concatenate([gate] * reps, axis=-1)
            return _dot(hid.astype(BF16), wd[...])

        y = expert(wga, wua, wda, lane_bcast(ga_ref))
        y_ref[...] = y + expert(wgb, wub, wdb, lane_bcast(gb_ref))

    @pl.when(valid_ref[pl.program_id(0)] == 0)
    def _():
        y_ref[...] = jnp.zeros_like(y_ref)


def _grouped_experts(layer, ea, eb, valid, fresh, xs, gas, gbs, w_gate, w_up, w_down):
    n_pad, d = xs.shape
    n_tiles = n_pad // TMOE
    de = w_gate.shape[3]
    wa = lambda r, c: pl.BlockSpec((None, None, r, c), lambda t, ea, eb, valid, fresh: (layer, ea[t], 0, 0))
    wb = lambda r, c: pl.BlockSpec((None, None, r, c), lambda t, ea, eb, valid, fresh: (layer, eb[t], 0, 0))
    gate_spec = pl.BlockSpec((None, 1, TMOE), lambda t, ea, eb, valid, fresh: (t, 0, 0))
    rows = pl.BlockSpec((TMOE, d), lambda t, ea, eb, valid, fresh: (t, 0))
    grid_spec = pltpu.PrefetchScalarGridSpec(
        num_scalar_prefetch=4, grid=(n_tiles,),
        in_specs=[rows, gate_spec, gate_spec, wa(d, de), wa(d, de), wa(de, d), wb(d, de), wb(d, de), wb(de, d)],
        out_specs=rows,
        scratch_shapes=[pltpu.VMEM((d, de), BF16), pltpu.VMEM((d, de), BF16), pltpu.VMEM((de, d), BF16)] * 2,
    )
    return pl.pallas_call(
        _moe_body, grid_spec=grid_spec, out_shape=jax.ShapeDtypeStruct((n_pad, d), F32),
        compiler_params=_cparams("arbitrary"), name="grouped_experts",
    )(ea, eb, valid, fresh, xs, gas.reshape(n_tiles, 1, TMOE), gbs.reshape(n_tiles, 1, TMOE),
      w_gate, w_up, w_down, w_gate, w_up, w_down)


def _rows(x, idx):
    return x.at[idx].get(mode="promise_in_bounds")


def _dispatch_plan(cls):
    n = cls.shape[0]
    n_pad = n + N_CLASSES * TMOE
    classes = jnp.arange(N_CLASSES, dtype=I32)
    onehot = (cls[:, None] == classes[None, :]).astype(I32)
    rank = jnp.sum(jnp.cumsum(onehot, axis=0) * onehot, axis=1) - 1
    counts = jnp.sum(onehot, axis=0)
    padded = (counts + TMOE - 1) // TMOE * TMOE
    ends = jnp.cumsum(padded)
    starts = ends - padded
    first = jnp.cumsum(counts) - counts
    pos = jnp.sum(onehot * starts[None, :], axis=1) + rank
    order = jnp.argsort(cls, stable=True).astype(I32)
    row = jnp.arange(n_pad, dtype=I32)
    row_cls = jnp.minimum(jnp.sum((ends[None, :] <= row[:, None]).astype(I32), axis=1), N_CLASSES - 1)
    row_hot = (row_cls[:, None] == classes[None, :]).astype(I32)
    within = row - jnp.sum(row_hot * starts[None, :], axis=1)
    live = within < jnp.sum(row_hot * counts[None, :], axis=1)
    src = _rows(order, jnp.clip(jnp.sum(row_hot * first[None, :], axis=1) + within, 0, n - 1))
    src = jnp.where(live, src, row % n)
    tile_cls = row_cls[::TMOE]
    valid = (row[::TMOE] < ends[-1]).astype(I32)
    pa = jnp.array([p[0] for p in PAIRS], I32)
    pb = jnp.array([p[1] for p in PAIRS], I32)
    grp = tile_cls // len(PAIRS)
    pair_hot = ((tile_cls % len(PAIRS))[:, None] == jnp.arange(len(PAIRS), dtype=I32)[None, :]).astype(I32)
    ea = grp * EPG + jnp.sum(pair_hot * pa[None, :], axis=1)
    eb = grp * EPG + jnp.sum(pair_hot * pb[None, :], axis=1)
    fresh = jnp.concatenate([jnp.ones((1,), I32), (tile_cls[1:] != tile_cls[:-1]).astype(I32)])
    return pos.astype(I32), src, live, ea.astype(I32), eb.astype(I32), valid, fresh


def _moe(layer, h1, cls, ga, gb, w_gate, w_up, w_down):
    pos, src, live, ea, eb, valid, fresh = _dispatch_plan(cls)
    xs = _rows(h1, src)
    gas = jnp.where(live, _rows(ga, src), 0.0)
    gbs = jnp.where(live, _rows(gb, src), 0.0)
    ys = _grouped_experts(layer, ea, eb, valid, fresh, xs, gas, gbs, w_gate, w_up, w_down)
    return _rows(ys, pos)


def _tail_body(alpha, h1_ref, y_ref, g_ref, b_ref, of_ref, ox_ref):
    out = _ln(alpha * h1_ref[...] + y_ref[...].astype(F32), g_ref[...], b_ref[...])
    of_ref[...] = out

    @pl.when(pl.program_id(0) == 0)
    def _():
        ox_ref[...] = out


def _tail(alpha, h1, y, g, b):
    n, d = h1.shape
    row = pl.BlockSpec((TM, d), lambda i: (i, 0))
    const = pl.BlockSpec((1, d), lambda i: (0, 0))
    return pl.pallas_call(
        functools.partial(_tail_body, alpha), grid=(n // TM,),
        in_specs=[row, row, const, const],
        out_specs=[pl.BlockSpec((TM, d), lambda i: (jnp.maximum(i - 1, 0), 0)),
                   pl.BlockSpec((TM, d), lambda i: (0, 0))],
        out_shape=[jax.ShapeDtypeStruct((n - TM, d), F32), jax.ShapeDtypeStruct((TM, d), F32)],
        compiler_params=_cparams("arbitrary"), name="final_norm",
    )(h1, y, g, b)


def kernel(x_prompt, x_sample, cache_k, cache_v, state_pool, meta_tokens, ln_in_g, ln_in_b, w_in, lambda_qk,
           subln_w, w_pool, pool_scale, w_out, ln1_g, ln1_b, router_w, router_b, w_gate, w_up, w_down,
           ln2_g, ln2_b):
    bsz, seq, d = x_prompt.shape
    n_dec, t_dec, _ = x_sample.shape
    depth = w_in.shape[0]
    past = cache_k.shape[2]
    n_heads = cache_k.shape[3]
    aw = n_heads * HEAD_W
    alpha = (2 * depth) ** 0.25
    n_sample = n_dec * t_dec
    assert seq % TM == 0 and n_sample + N_META <= TM and t_dec == N_META and n_sample % POOL_HALO == 0
    assert past % CHUNK == 0 and t_dec <= CHUNK

    x_extra = jnp.concatenate([x_sample.reshape(n_sample, d), meta_tokens.astype(F32),
                               jnp.zeros((TM - n_sample - N_META, d), F32)], axis=0)
    x_frames = x_prompt.reshape(bsz * seq, d)
    row = lambda v: v.reshape(1, -1).astype(F32)
    slopes = (2.0 ** (-8.0 * jnp.arange(1, n_heads + 1, dtype=F32) / n_heads)).astype(F32)
    bias, diag = _alibi_tables(slopes, seq)
    rw_hi = _bf16_part(router_w.astype(F32))
    rw_lo = router_w.astype(F32) - rw_hi
    r1 = jnp.concatenate([rw_hi, rw_lo, jnp.zeros((d, HEAD_W - 2 * N_EXPERTS), F32)], axis=1).astype(BF16)
    rb = router_b.astype(F32)

    k_out = v_out = None
    h1 = y = None
    pools_p, pools_s, ks_s, vs_s = [], [], [], []
    for l in range(depth):
        lam_init = 0.8 - 0.6 * math.exp(-0.3 * l)
        lq = lambda_qk[l].astype(F32)
        lam = (jnp.exp(jnp.sum(lq[0] * lq[1])) - jnp.exp(jnp.sum(lq[2] * lq[3])) + lam_init).reshape(1)
        sw = row(subln_w[l]) * (1.0 - lam_init)
        w_in_l = w_in[l].astype(BF16)
        if l == 0:
            k1_outs = _k1_first(x_extra, x_frames, row(ln_in_g), row(ln_in_b), w_in_l, depth, bsz, seq)
        else:
            k1_outs = _k1_next(l, alpha, h1, y, row(ln2_g[l - 1]), row(ln2_b[l - 1]), w_in_l, k_out, v_out, seq)
        h, q, u, kb, vb, k_out, v_out, kx, vx = k1_outs
        meta_shape = (bsz, N_META, n_heads, HEAD_W)
        meta_k = jnp.broadcast_to(kx[n_sample:n_sample + N_META].reshape(1, *meta_shape[1:]), meta_shape)
        meta_v = jnp.broadcast_to(vx[n_sample:n_sample + N_META].reshape(1, *meta_shape[1:]), meta_shape)
        k_out = k_out.at[l, :, :N_META].set(meta_k)
        v_out = v_out.at[l, :, :N_META].set(meta_v)
        a = _prompt_attention(lam, q, kb, vb, kx, vx, bias, diag, sw, bsz, seq, n_sample)
        wp = w_pool[l].astype(BF16)
        ps = row(pool_scale[l])
        wo = w_out[l].astype(BF16)
        g1, b1 = row(ln1_g[l]), row(ln1_b[l])
        *x_outs, npool = _extras(l, alpha, slopes, lam, rb, q, kx, vx, u, cache_k, cache_v, state_pool, sw, wp, ps, h, wo,
                                 g1, b1, r1, n_dec, t_dec)
        h1, cls, ga, gb = _frames_mix(alpha, rb, a, u, wp, ps, h, wo, g1, b1, r1, x_outs, seq, n_sample)
        y = _moe(l, h1, cls.reshape(-1), ga.reshape(-1), gb.reshape(-1), w_gate, w_up, w_down)
        keep = state_pool.shape[2]
        last = (TM + (jnp.arange(bsz, dtype=I32)[:, None] + 1) * seq - keep + jnp.arange(keep, dtype=I32)[None, :])
        pools_p.append(_rows(u, last.reshape(-1)).reshape(bsz, keep, -1))
        pools_s.append(npool[:n_dec])
        ks_s.append(kx[:n_sample].reshape(n_dec, t_dec, n_heads, HEAD_W))
        vs_s.append(vx[:n_sample].reshape(n_dec, t_dec, n_heads, HEAD_W))

    y_frames, y_extra = _tail(alpha, h1, y, row(ln2_g[depth - 1]), row(ln2_b[depth - 1]))
    return (y_frames.reshape(bsz, seq, d), y_extra[:n_sample].reshape(n_dec, t_dec, d), k_out, v_out,
            jnp.stack(pools_p), jnp.stack(ks_s), jnp.stack(vs_s), jnp.stack(pools_s))
```

```python
import functools
import math

import jax
import jax.numpy as jnp
from jax import lax
from jax.experimental import pallas as pl
from jax.experimental.pallas import tpu as pltpu

F32, BF16, I32 = jnp.float32, jnp.bfloat16, jnp.int32

CHUNK = 64
N_META = 16
HEAD_DIM = 64
HEAD_W = 2 * HEAD_DIM
POOL_WINDOWS = (2, 4, 8, 16)
POOL_HALO = 16
N_EXPERTS = 16
N_GROUPS = 4
EPG = N_EXPERTS // N_GROUPS
PAIRS = ((0, 1), (0, 2), (0, 3), (1, 2), (1, 3), (2, 3))
N_CLASSES = N_GROUPS * len(PAIRS)
LN_EPS = 1e-5
NEG = -1e30
LOG2E = 1.4426950408889634
Q_SCALE = HEAD_DIM ** -0.5 * LOG2E

TM = 512
TMOE = 256
ATT_T = 256
ATT_PAD = 128
VMEM_LIMIT = 56 * 1024 * 1024


def _cparams(*sem):
    return pltpu.CompilerParams(dimension_semantics=sem, vmem_limit_bytes=VMEM_LIMIT)


def _ln(x, g, b):
    mu = jnp.mean(x, axis=-1, keepdims=True)
    xc = x - mu
    var = jnp.mean(xc * xc, axis=-1, keepdims=True)
    return xc * lax.rsqrt(var + LN_EPS) * g + b


def _dot(a, b):
    return jnp.dot(a, b, preferred_element_type=F32)


def _dot_nt(a, b):
    return lax.dot_general(a, b, (((1,), (1,)), ((), ())), preferred_element_type=F32)


def _k1_project(norm, w_ref, h_ref, q_ref, u_ref, kb_ref, vb_ref, k_ref, v_ref, kx_ref, vx_ref, is_extras):
    aw = kx_ref.shape[-1]
    half = TM // 2
    for r0 in (0, half):
        rows = slice(r0, r0 + half)
        hb = norm(rows).astype(BF16)
        h_ref[rows, :] = hb
        q_ref[rows, :] = (_dot(hb, w_ref[:, 0:aw]) * Q_SCALE).astype(BF16)
        k = _dot(hb, w_ref[:, aw:2 * aw])
        v = _dot(hb, w_ref[:, 2 * aw:3 * aw])
        u_ref[rows, :] = _dot(hb, w_ref[:, 3 * aw:])
        kb_ref[rows, :] = k.astype(BF16)
        vb_ref[rows, :] = v.astype(BF16)
        k_ref[rows] = k.reshape((half,) + k_ref.shape[1:])
        v_ref[rows] = v.reshape((half,) + v_ref.shape[1:])
        if is_extras:
            kx_ref[rows, :] = k
            vx_ref[rows, :] = v


def _k1_first_body(xe_ref, x_ref, g_ref, b_ref, w_ref, h_ref, q_ref, u_ref, kb_ref, vb_ref, k_ref, v_ref, kx_ref,
                   vx_ref):
    outs = (w_ref, h_ref, q_ref, u_ref, kb_ref, vb_ref, k_ref, v_ref, kx_ref, vx_ref)

    @pl.when(pl.program_id(0) == 0)
    def _():
        _k1_project(lambda rows: _ln(xe_ref[rows, :], g_ref[...], b_ref[...]), *outs, True)

    @pl.when(pl.program_id(0) > 0)
    def _():
        _k1_project(lambda rows: _ln(x_ref[rows, :], g_ref[...], b_ref[...]), *outs, False)


def _k1_next_body(alpha, h1_ref, y_ref, g_ref, b_ref, w_ref, ko_ref, vo_ref,
                  h_ref, q_ref, u_ref, kb_ref, vb_ref, k_ref, v_ref, kx_ref, vx_ref):
    del ko_ref, vo_ref
    outs = (w_ref, h_ref, q_ref, u_ref, kb_ref, vb_ref, k_ref, v_ref, kx_ref, vx_ref)

    def norm(rows):
        return _ln(alpha * h1_ref[rows, :] + y_ref[rows, :].astype(F32), g_ref[...], b_ref[...])

    @pl.when(pl.program_id(0) == 0)
    def _():
        _k1_project(norm, *outs, True)

    @pl.when(pl.program_id(0) > 0)
    def _():
        _k1_project(norm, *outs, False)


def _k1_specs(layer, n_tiles, tiles_per_b, d, aw, pw):
    def frames_tile(i):
        return jnp.maximum(i - 1, 0)

    def kv_map(i):
        t = frames_tile(i)
        return (layer, t // tiles_per_b, pl.multiple_of(N_META + (t % tiles_per_b) * TM, 16), 0, 0)

    kv_spec = pl.BlockSpec((None, None, pl.Element(TM), pl.Element(aw // HEAD_W), pl.Element(HEAD_W)), kv_map)
    row = lambda w: pl.BlockSpec((TM, w), lambda i: (i, 0))
    const = lambda r, w: pl.BlockSpec((r, w), lambda i: (0, 0))
    out_specs = [row(d), row(aw), row(pw), row(aw), row(aw), kv_spec, kv_spec, const(TM, aw), const(TM, aw)]
    return row, const, frames_tile, out_specs


def _k1_out_shapes(n, d, aw, pw, depth, b, l):
    return [
        jax.ShapeDtypeStruct((n, d), BF16),
        jax.ShapeDtypeStruct((n, aw), BF16),
        jax.ShapeDtypeStruct((n, pw), F32),
        jax.ShapeDtypeStruct((n, aw), BF16),
        jax.ShapeDtypeStruct((n, aw), BF16),
        jax.ShapeDtypeStruct((depth, b, l, aw // HEAD_W, HEAD_W), F32),
        jax.ShapeDtypeStruct((depth, b, l, aw // HEAD_W, HEAD_W), F32),
        jax.ShapeDtypeStruct((TM, aw), F32),
        jax.ShapeDtypeStruct((TM, aw), F32),
    ]


def _k1_first(x_extra, x_frames, g, b, w, depth, bsz, seq):
    nf, d = x_frames.shape
    aw = w.shape[1] // 4
    pw = w.shape[1] - 3 * aw
    n = TM + nf
    n_tiles = n // TM
    tpb = seq // TM
    row, const, frames_tile, out_specs = _k1_specs(0, n_tiles, tpb, d, aw, pw)
    in_specs = [const(TM, d), pl.BlockSpec((TM, d), lambda i: (frames_tile(i), 0)),
                const(1, d), const(1, d), const(d, w.shape[1])]
    return pl.pallas_call(
        _k1_first_body, grid=(n_tiles,), in_specs=in_specs, out_specs=out_specs,
        out_shape=_k1_out_shapes(n, d, aw, pw, depth, bsz, N_META + seq),
        compiler_params=_cparams("arbitrary"), name="k1_first",
    )(x_extra, x_frames, g, b, w)


def _k1_next(layer, alpha, h1, y, g, b, w, k_out, v_out, seq):
    n, d = h1.shape
    depth, bsz, l, n_heads, _ = k_out.shape
    aw = n_heads * HEAD_W
    pw = w.shape[1] - 3 * aw
    n_tiles = n // TM
    tpb = seq // TM
    row, const, _, out_specs = _k1_specs(layer, n_tiles, tpb, d, aw, pw)
    any_spec = pl.BlockSpec(memory_space=pl.ANY)
    in_specs = [row(d), row(d), const(1, d), const(1, d), const(d, w.shape[1]), any_spec, any_spec]
    return pl.pallas_call(
        functools.partial(_k1_next_body, alpha), grid=(n_tiles,), in_specs=in_specs, out_specs=out_specs,
        out_shape=_k1_out_shapes(n, d, aw, pw, depth, bsz, l),
        input_output_aliases={5: 5, 6: 6},
        compiler_params=_cparams("arbitrary"), name="k1_next",
    )(h1, y, g, b, w, k_out, v_out)


def _bf16_part(x):
    bits = lax.bitcast_convert_type(x, jnp.uint32) & jnp.uint32(0xFFFF0000)
    return lax.bitcast_convert_type(bits, F32)


def _split3(c):
    hi = _bf16_part(c)
    r = c - hi
    mid = _bf16_part(r)
    return hi, mid, r - mid


def _alibi_tables(slopes, seq):
    pad = ATT_PAD - N_META
    pos = jnp.concatenate([jnp.zeros((pad,), F32), jnp.arange(N_META + seq, dtype=F32)])
    live = jnp.arange(ATT_PAD + seq) >= pad
    c = jnp.where(live[None, :], (slopes * LOG2E)[:, None] * pos[None, :], NEG)
    bias = jnp.pad(jnp.stack(_split3(c), axis=-1), ((0, 0), (0, 0), (0, HEAD_W - 3))).astype(BF16)
    rr = lax.broadcasted_iota(I32, (ATT_T, ATT_T), 0)
    cc = lax.broadcasted_iota(I32, (ATT_T, ATT_T), 1)
    ahead = jnp.maximum(cc - rr, 0).astype(F32)
    diag = jnp.where(((cc // CHUNK) <= (rr // CHUNK))[None], (-2.0 * LOG2E * slopes)[:, None, None] * ahead[None], NEG)
    return bias, diag


def _attn_body(lam_ref, q_ref, kb_ref, vb_ref, kx_ref, vx_ref, bias_ref, diag_ref, sw_ref, o_ref, ka, vm, qa1, qa2,
               sc):
    t = ATT_T
    lam = lam_ref[0]
    s_len = q_ref.shape[0]
    n_sub = s_len // t
    kp = ATT_PAD
    pad = kp - N_META
    lane = lax.broadcasted_iota(I32, (1, HEAD_W), 1)
    first_half = (lane < HEAD_DIM).astype(BF16)
    ones = jnp.broadcast_to((lane < 3).astype(BF16), (s_len, HEAD_W))

    ka[0:pad, 0:HEAD_W] = jnp.zeros((pad, HEAD_W), BF16)
    ka[pad:kp, 0:HEAD_W] = kx_ref[...].astype(BF16)
    ka[kp:, 0:HEAD_W] = kb_ref[...]
    ka[:, HEAD_W:] = bias_ref[...]
    vm[0:pad, :] = jnp.zeros((pad, HEAD_W), BF16)
    vm[pad:kp, :] = vx_ref[...].astype(BF16)
    qb = q_ref[...]
    qa1[:, 0:HEAD_W] = qb * first_half
    qa2[:, 0:HEAD_W] = qb * (1 - first_half)
    qa1[:, HEAD_W:] = ones
    qa2[:, HEAD_W:] = ones

    def tiles(sb):
        return [(0, kp)] + [(kp + j * t, t) for j in range(sb + 1)]

    def fold(x, op):
        return x if x.shape[1] == HEAD_W else op(x[:, :HEAD_W], x[:, HEAD_W:])

    def scores(sb):
        row_max = []
        for mp, qa in enumerate((qa1, qa2)):
            q = qa[sb * t:(sb + 1) * t, :]
            macc = None
            tl = tiles(sb)
            for i, (c0, w) in enumerate(tl):
                s = _dot_nt(q, ka[c0:c0 + w, :])
                if i == len(tl) - 1:
                    s = s + diag_ref[...]
                sc[sb % 2, mp, :, c0:c0 + w] = s
                f = fold(s, jnp.maximum)
                macc = f if macc is None else jnp.maximum(macc, f)
            row_max.append(jnp.max(macc, axis=-1, keepdims=True))
        return row_max

    nxt = scores(0)
    for sb in range(n_sub):
        q0 = sb * t
        row_max = nxt
        if sb + 1 < n_sub:
            nxt = scores(sb + 1)
        ls = []
        for mp in range(2):
            mb = jnp.broadcast_to(row_max[mp], (t, HEAD_W))
            lacc = None
            for c0, w in tiles(sb):
                s = sc[sb % 2, mp, :, c0:c0 + w]
                p = jnp.exp2(s - (mb if w == HEAD_W else jnp.concatenate([mb, mb], axis=-1)))
                sc[sb % 2, mp, :, c0:c0 + w] = p
                f = fold(p, jnp.add)
                lacc = f if lacc is None else lacc + f
            ls.append(jnp.sum(lacc, axis=-1, keepdims=True))
        fac = lam * ls[0] / ls[1]
        acc = None
        for c0, w in tiles(sb):
            wt = (sc[sb % 2, 0, :, c0:c0 + w] - sc[sb % 2, 1, :, c0:c0 + w] * fac).astype(BF16)
            d = _dot(wt, vm[...] if c0 == 0 else vb_ref[c0 - kp:c0 - kp + w, :])
            acc = d if acc is None else acc + d
        o = acc / ls[0]
        o = o * lax.rsqrt(jnp.mean(o * o, axis=-1, keepdims=True) + LN_EPS)
        o_ref[q0:q0 + t, :] = (o * sw_ref[...]).astype(BF16)


def _prompt_attention(lam, q, kb, vb, kx, vx, bias, diag, sw, bsz, seq, meta_row):
    n, aw = q.shape
    n_heads = aw // HEAD_W
    key_rows = ATT_PAD + seq
    smem = pl.BlockSpec(memory_space=pltpu.SMEM)
    frames = pl.BlockSpec((pl.Element(seq), pl.Element(HEAD_W)),
                          lambda b, h: (pl.multiple_of(TM + b * seq, TM), pl.multiple_of(h * HEAD_W, HEAD_W)))
    meta = pl.BlockSpec((N_META, HEAD_W), lambda b, h: (meta_row // N_META, h))
    return pl.pallas_call(
        _attn_body, grid=(bsz, n_heads),
        in_specs=[smem, frames, frames, frames, meta, meta,
                  pl.BlockSpec((None, key_rows, HEAD_W), lambda b, h: (h, 0, 0)),
                  pl.BlockSpec((None, ATT_T, ATT_T), lambda b, h: (h, 0, 0)),
                  pl.BlockSpec((1, HEAD_W), lambda b, h: (0, 0))],
        out_specs=frames,
        out_shape=jax.ShapeDtypeStruct((n, aw), BF16),
        scratch_shapes=[pltpu.VMEM((key_rows, 2 * HEAD_W), BF16), pltpu.VMEM((ATT_PAD, HEAD_W), BF16),
                        pltpu.VMEM((seq, 2 * HEAD_W), BF16), pltpu.VMEM((seq, 2 * HEAD_W), BF16),
                        pltpu.VMEM((2, 2, ATT_T, key_rows), F32)],
        compiler_params=_cparams("arbitrary", "arbitrary"), name="prompt_attention",
    )(lam, q, kb, vb, kx, vx, bias, diag, sw)


def _pool_mix(ext_ref, n_rows, counts, wp_ref, ps_ref, row0=0):
    gw = wp_ref.shape[-1]
    outs = []
    base = POOL_HALO + row0
    for g, w in enumerate(POOL_WINDOWS):
        cols = slice(g * gw, (g + 1) * gw)
        tok = ext_ref[base:base + n_rows, cols]
        acc = tok
        for i in range(1, w):
            acc = acc + ext_ref[base - i:base - i + n_rows, cols]
        if counts is None:
            d = acc * (1.0 / w) - tok
        else:
            d = acc / jnp.minimum(counts, float(w)) - tok
        outs.append(_dot(d.astype(BF16), wp_ref[g]))
    return jnp.concatenate(outs, axis=-1) * ps_ref[...]


def _route(h1, r1_ref, rb_ref, cls_ref, ga_ref, gb_ref):
    hi = h1.astype(BF16)
    lo = (h1 - hi.astype(F32)).astype(BF16)
    at = _dot(hi, r1_ref[...]).T
    bt = _dot(lo, r1_ref[...]).T
    logit = at[0:N_EXPERTS] + at[N_EXPERTS:2 * N_EXPERTS] + bt[0:N_EXPERTS]
    lg = [logit[e:e + 1, :] for e in range(N_EXPERTS)]
    m = functools.reduce(jnp.maximum, lg)
    ex = [jnp.exp(x - m) for x in lg]
    z = functools.reduce(jnp.add, ex)
    prob = [x / z for x in ex]
    sel = [prob[e] + rb_ref[e] for e in range(N_EXPERTS)]

    def top2_sum(v):
        best = None
        for a, b in PAIRS:
            s = v[a] + v[b]
            best = s if best is None else jnp.maximum(best, s)
        return best

    score = [top2_sum(sel[g * EPG:(g + 1) * EPG]) for g in range(N_GROUPS)]
    gi = jnp.zeros_like(score[0], dtype=I32)
    best = score[0]
    for g in range(1, N_GROUPS):
        better = score[g] > best
        gi = jnp.where(better, g, gi)
        best = jnp.where(better, score[g], best)

    def pick(vals, j):
        out = vals[j]
        for g in range(1, N_GROUPS):
            out = jnp.where(gi == g, vals[g * EPG + j], out)
        return out

    sg = [pick(sel, j) for j in range(EPG)]
    pg = [pick(prob, j) for j in range(EPG)]

    def argmax_first(v, skip=None):
        bi = jnp.zeros_like(gi)
        bv = None
        for j in range(EPG):
            x = v[j] if skip is None else jnp.where(skip == j, -jnp.inf, v[j])
            if bv is None:
                bv = x
            else:
                better = x > bv
                bi = jnp.where(better, j, bi)
                bv = jnp.where(better, x, bv)
        return bi

    e1 = argmax_first(sg)
    e2 = argmax_first(sg, skip=e1)

    def at_idx(v, idx):
        out = v[0]
        for j in range(1, EPG):
            out = jnp.where(idx == j, v[j], out)
        return out

    g1 = at_idx(pg, e1)
    g2 = at_idx(pg, e2)
    tot = g1 + g2
    g1 = g1 / tot
    g2 = g2 / tot
    swap = e2 < e1
    ea = jnp.where(swap, e2, e1)
    eb = jnp.where(swap, e1, e2)
    pair = jnp.zeros_like(gi)
    for p, (a, b) in enumerate(PAIRS):
        pair = jnp.where((ea == a) & (eb == b), p, pair)
    cls_ref[...] = gi * len(PAIRS) + pair
    ga_ref[...] = jnp.where(swap, g2, g1)
    gb_ref[...] = jnp.where(swap, g1, g2)


def _post(alpha, mix, h, wo_ref, g_ref, b_ref, r1_ref, rb_ref, h1_ref, cls_ref, ga_ref, gb_ref):
    h1 = _ln(alpha * h.astype(F32) + _dot(mix, wo_ref[...]), g_ref[...], b_ref[...])
    h1_ref[...] = h1
    _route(h1, r1_ref, rb_ref, cls_ref, ga_ref, gb_ref)


def _post_out_shapes(n, d):
    nt = n // TM
    return [jax.ShapeDtypeStruct((n, d), F32), jax.ShapeDtypeStruct((nt, 1, TM), I32),
            jax.ShapeDtypeStruct((nt, 1, TM), F32), jax.ShapeDtypeStruct((nt, 1, TM), F32)]


def _kx_body(alpha, n_dec, t_dec, slope_ref, lam_ref, rb_ref, q_ref, kx_ref, vx_ref, ux_ref, ck_ref, cv_ref,
             sp_ref, sw_ref, wp_ref, ps_ref, h_ref, wo_ref, g_ref, b_ref, r1_ref,
             h1_ref, cls_ref, ga_ref, gb_ref, npool_ref, mix, ext):
    s = pl.program_id(0)
    past = ck_ref.shape[0]
    aw = q_ref.shape[1]
    n_heads = aw // HEAD_W
    is_meta = s == n_dec
    lam = lam_ref[0]
    r0 = pl.multiple_of(s * t_dec, t_dec)

    @pl.when(s == 0)
    def _():
        mix[...] = jnp.zeros_like(mix)

    q = q_ref[pl.ds(r0, t_dec), :].astype(F32)
    kn = kx_ref[pl.ds(r0, t_dec), :]
    vn = vx_ref[pl.ds(r0, t_dec), :]
    rq = lax.broadcasted_iota(I32, (t_dec, past), 0)
    cp = lax.broadcasted_iota(I32, (t_dec, past), 1)
    dist_p = (past + rq - cp).astype(F32)
    rn = lax.broadcasted_iota(I32, (t_dec, t_dec), 0)
    cn = lax.broadcasted_iota(I32, (t_dec, t_dec), 1)
    dist_n = jnp.abs(rn - cn).astype(F32)
    heads = []
    for hd in range(n_heads):
        slope = slope_ref[hd] * LOG2E
        c0 = hd * HEAD_W
        ckh = ck_ref[:, hd, :].astype(BF16)
        cvh = cv_ref[:, hd, :].astype(BF16)
        maps = []
        for half in range(2):
            lo_c = c0 + half * HEAD_DIM
            qh = q[:, lo_c:lo_c + HEAD_DIM].astype(BF16)
            sp = _dot_nt(qh, ckh[:, half * HEAD_DIM:(half + 1) * HEAD_DIM]) - slope * dist_p
            sp = jnp.where(is_meta, NEG, sp)
            sn = _dot_nt(qh, kn[:, lo_c:lo_c + HEAD_DIM].astype(BF16)) - slope * dist_n
            m = jnp.maximum(jnp.max(sp, axis=-1, keepdims=True), jnp.max(sn, axis=-1, keepdims=True))
            pp = jnp.exp2(sp - m)
            pn = jnp.exp2(sn - m)
            z = jnp.sum(pp, axis=-1, keepdims=True) + jnp.sum(pn, axis=-1, keepdims=True)
            maps.append((pp / z, pn / z))
        wp_ = (maps[0][0] - lam * maps[1][0]).astype(BF16)
        wn_ = (maps[0][1] - lam * maps[1][1]).astype(BF16)
        o = _dot(wp_, cvh) + _dot(wn_, vn[:, c0:c0 + HEAD_W].astype(BF16))
        o = o * lax.rsqrt(jnp.mean(o * o, axis=-1, keepdims=True) + LN_EPS)
        heads.append(o * sw_ref[...])
    attn = jnp.concatenate(heads, axis=-1)

    hist = jnp.where(is_meta, 0.0, sp_ref[...])
    u = ux_ref[pl.ds(r0, t_dec), :]
    ext[POOL_HALO - sp_ref.shape[0]:POOL_HALO, :] = hist
    ext[POOL_HALO:POOL_HALO + t_dec, :] = u
    pos = lax.broadcasted_iota(I32, (t_dec, 1), 0) + jnp.where(is_meta, 0, past)
    pool = _pool_mix(ext, t_dec, (pos + 1).astype(F32), wp_ref, ps_ref)
    mix[pl.ds(r0, t_dec), :] = jnp.concatenate([attn, pool], axis=-1).astype(BF16)
    keep = sp_ref.shape[0]
    npool_ref[...] = ext[POOL_HALO + t_dec - keep:POOL_HALO + t_dec, :]

    @pl.when(s == n_dec)
    def _():
        _post(alpha, mix[...], h_ref[...], wo_ref, g_ref, b_ref, r1_ref, rb_ref,
              h1_ref, cls_ref, ga_ref, gb_ref)


def _extras(layer, alpha, slopes, lam, rb, q, kx, vx, u, cache_k, cache_v, state_pool, sw, wp, ps, h, wo, g, b,
            r1, n_dec, t_dec):
    n, d = h.shape
    aw = q.shape[1]
    pw = u.shape[1]
    past = cache_k.shape[2]
    keep = state_pool.shape[2]
    smem = pl.BlockSpec(memory_space=pltpu.SMEM)
    tile0 = lambda w: pl.BlockSpec((TM, w), lambda s: (0, 0))
    full = lambda *shape: pl.BlockSpec(shape, lambda s: (0,) * len(shape))
    dec = lambda s: jnp.minimum(s, n_dec - 1)
    cache_spec = pl.BlockSpec((None, None, past, aw // HEAD_W, HEAD_W), lambda s: (layer, dec(s), 0, 0, 0))
    small = pl.BlockSpec((None, 1, TM), lambda s: (0, 0, 0))
    out_shapes = _post_out_shapes(n, d) + [jax.ShapeDtypeStruct((n_dec + 1, keep, pw), F32)]
    out_specs = [tile0(d), small, small, small, pl.BlockSpec((None, keep, pw), lambda s: (s, 0, 0))]
    return pl.pallas_call(
        functools.partial(_kx_body, alpha, n_dec, t_dec), grid=(n_dec + 1,),
        in_specs=[smem, smem, smem, tile0(aw), full(TM, aw), full(TM, aw), tile0(pw), cache_spec, cache_spec,
                  pl.BlockSpec((None, None, keep, pw), lambda s: (layer, dec(s), 0, 0)),
                  full(1, HEAD_W), full(*wp.shape), full(1, pw), tile0(d), full(*wo.shape), full(1, d), full(1, d),
                  full(*r1.shape)],
        out_specs=out_specs, out_shape=out_shapes,
        scratch_shapes=[pltpu.VMEM((TM, aw + pw), BF16), pltpu.VMEM((POOL_HALO + t_dec, pw), F32)],
        compiler_params=_cparams("arbitrary"), name="extras_mix",
    )(slopes, lam, rb, q, kx, vx, u, cache_k, cache_v, state_pool, sw, wp, ps, h, wo, g, b, r1)


def _k3_body(alpha, tiles_per_b, rb_ref, a_ref, u_ref, uprev_ref, umeta_ref, wp_ref, ps_ref, h_ref, wo_ref, g_ref,
             b_ref, r1_ref, h1_in, cls_in, ga_in, gb_in, h1_ref, cls_ref, ga_ref, gb_ref, ext):
    del h1_in, cls_in, ga_in, gb_in
    first = (pl.program_id(0) % tiles_per_b) == 0
    ext[0:POOL_HALO, :] = jnp.where(first, umeta_ref[...], uprev_ref[...])
    ext[POOL_HALO:, :] = u_ref[...]
    half = TM // 2
    pre = []
    for r0 in (0, half):
        rows = slice(r0, r0 + half)
        pool = _pool_mix(ext, half, None, wp_ref, ps_ref, row0=r0)
        mix = jnp.concatenate([a_ref[rows, :], pool.astype(BF16)], axis=-1)
        pre.append(alpha * h_ref[rows, :].astype(F32) + _dot(mix, wo_ref[...]))
    for r0, z in zip((0, half), pre):
        rows = slice(r0, r0 + half)
        h1 = _ln(z, g_ref[...], b_ref[...])
        h1_ref[rows, :] = h1
        _route(h1, r1_ref, rb_ref, cls_ref.at[:, rows], ga_ref.at[:, rows], gb_ref.at[:, rows])


def _frames_mix(alpha, rb, a, u, wp, ps, h, wo, g, b, r1, prev_outs, seq, meta_row):
    n, d = h.shape
    aw, pw = a.shape[1], u.shape[1]
    n_tiles = n // TM - 1
    tpb = seq // TM
    smem = pl.BlockSpec(memory_space=pltpu.SMEM)
    row = lambda w: pl.BlockSpec((TM, w), lambda i: (i + 1, 0))
    full = lambda *shape: pl.BlockSpec(shape, lambda i: (0,) * len(shape))
    hb = TM // POOL_HALO
    small = pl.BlockSpec((None, 1, TM), lambda i: (i + 1, 0, 0))
    any_spec = pl.BlockSpec(memory_space=pl.ANY)
    return pl.pallas_call(
        functools.partial(_k3_body, alpha, tpb), grid=(n_tiles,),
        in_specs=[smem, row(aw), row(pw),
                  pl.BlockSpec((POOL_HALO, pw), lambda i: ((i + 1) * hb - 1, 0)),
                  pl.BlockSpec((POOL_HALO, pw), lambda i: (meta_row // POOL_HALO, 0)),
                  full(*wp.shape), full(1, pw), row(d), full(*wo.shape), full(1, d), full(1, d), full(*r1.shape)]
                 + [any_spec] * 4,
        out_specs=[row(d), small, small, small],
        out_shape=_post_out_shapes(n, d),
        input_output_aliases={12: 0, 13: 1, 14: 2, 15: 3},
        scratch_shapes=[pltpu.VMEM((POOL_HALO + TM, pw), F32)],
        compiler_params=_cparams("arbitrary"), name="frames_mix",
    )(rb, a, u, u, u, wp, ps, h, wo, g, b, r1, *prev_outs)


def _moe_body(ea_ref, eb_ref, valid_ref, fresh_ref, x_ref, ga_ref, gb_ref, wga_f, wua_f, wda_f, wgb_f, wub_f, wdb_f,
              y_ref, wga, wua, wda, wgb, wub, wdb):
    del ea_ref, eb_ref

    @pl.when(fresh_ref[pl.program_id(0)] > 0)
    def _():
        for src, dst in ((wga_f, wga), (wua_f, wua), (wda_f, wda), (wgb_f, wgb), (wub_f, wub), (wdb_f, wdb)):
            dst[...] = src[...].astype(BF16)

    @pl.when(valid_ref[pl.program_id(0)] > 0)
    def _():
        x = x_ref[...].astype(BF16)
        rows = x.shape[0]

        def lane_bcast(g_ref):
            return jnp.broadcast_to(g_ref[...], (HEAD_W, rows)).T

        def expert(wg, wu, wd, gate):
            gt = _dot(x, wg[...])
            hid = gt * (1.0 / (1.0 + jnp.exp(-gt))) * _dot(x, wu[...])
            reps = hid.shape[1] // HEAD_W
            hid = hid * jnp.concatenate([gate] * reps, axis=-1)
            return _dot(hid.astype(BF16), wd[...])

        y = expert(wga, wua, wda, lane_bcast(ga_ref))
        y_ref[...] = (y + expert(wgb, wub, wdb, lane_bcast(gb_ref))).astype(y_ref.dtype)

    @pl.when(valid_ref[pl.program_id(0)] == 0)
    def _():
        y_ref[...] = jnp.zeros_like(y_ref)


def _grouped_experts(layer, ea, eb, valid, fresh, xs, gas, gbs, w_gate, w_up, w_down):
    n_pad, d = xs.shape
    n_tiles = n_pad // TMOE
    de = w_gate.shape[3]
    wa = lambda r, c: pl.BlockSpec((None, None, r, c), lambda t, ea, eb, valid, fresh: (layer, ea[t], 0, 0))
    wb = lambda r, c: pl.BlockSpec((None, None, r, c), lambda t, ea, eb, valid, fresh: (layer, eb[t], 0, 0))
    gate_spec = pl.BlockSpec((None, 1, TMOE), lambda t, ea, eb, valid, fresh: (t, 0, 0))
    rows = pl.BlockSpec((TMOE, d), lambda t, ea, eb, valid, fresh: (t, 0))
    grid_spec = pltpu.PrefetchScalarGridSpec(
        num_scalar_prefetch=4, grid=(n_tiles,),
        in_specs=[rows, gate_spec, gate_spec, wa(d, de), wa(d, de), wa(de, d), wb(d, de), wb(d, de), wb(de, d)],
        out_specs=rows,
        scratch_shapes=[pltpu.VMEM((d, de), BF16), pltpu.VMEM((d, de), BF16), pltpu.VMEM((de, d), BF16)] * 2,
    )
    return pl.pallas_call(
        _moe_body, grid_spec=grid_spec, out_shape=jax.ShapeDtypeStruct((n_pad, d), BF16),
        compiler_params=_cparams("arbitrary"), name="grouped_experts",
    )(ea, eb, valid, fresh, xs, gas.reshape(n_tiles, 1, TMOE), gbs.reshape(n_tiles, 1, TMOE),
      w_gate, w_up, w_down, w_gate, w_up, w_down)


def _rows(x, idx):
    return x.at[idx].get(mode="promise_in_bounds")


def _dispatch_plan(cls):
    n = cls.shape[0]
    n_pad = n + N_CLASSES * TMOE
    classes = jnp.arange(N_CLASSES, dtype=I32)
    onehot = (cls[:, None] == classes[None, :]).astype(I32)
    rank = jnp.sum(jnp.cumsum(onehot, axis=0) * onehot, axis=1) - 1
    counts = jnp.sum(onehot, axis=0)
    padded = (counts + TMOE - 1) // TMOE * TMOE
    ends = jnp.cumsum(padded)
    starts = ends - padded
    first = jnp.cumsum(counts) - counts
    pos = jnp.sum(onehot * starts[None, :], axis=1) + rank
    order = jnp.argsort(cls, stable=True).astype(I32)
    row = jnp.arange(n_pad, dtype=I32)
    row_cls = jnp.minimum(jnp.sum((ends[None, :] <= row[:, None]).astype(I32), axis=1), N_CLASSES - 1)
    row_hot = (row_cls[:, None] == classes[None, :]).astype(I32)
    within = row - jnp.sum(row_hot * starts[None, :], axis=1)
    live = within < jnp.sum(row_hot * counts[None, :], axis=1)
    src = _rows(order, jnp.clip(jnp.sum(row_hot * first[None, :], axis=1) + within, 0, n - 1))
    src = jnp.where(live, src, row % n)
    tile_cls = row_cls[::TMOE]
    valid = (row[::TMOE] < ends[-1]).astype(I32)
    pa = jnp.array([p[0] for p in PAIRS], I32)
    pb = jnp.array([p[1] for p in PAIRS], I32)
    grp = tile_cls // len(PAIRS)
    pair_hot = ((tile_cls % len(PAIRS))[:, None] == jnp.arange(len(PAIRS), dtype=I32)[None, :]).astype(I32)
    ea = grp * EPG + jnp.sum(pair_hot * pa[None, :], axis=1)
    eb = grp * EPG + jnp.sum(pair_hot * pb[None, :], axis=1)
    fresh = jnp.concatenate([jnp.ones((1,), I32), (tile_cls[1:] != tile_cls[:-1]).astype(I32)])
    return pos.astype(I32), src, live, ea.astype(I32), eb.astype(I32), valid, fresh


def _moe(layer, h1, cls, ga, gb, w_gate, w_up, w_down):
    pos, src, live, ea, eb, valid, fresh = _dispatch_plan(cls)
    xs = _rows(h1, src)
    gas = jnp.where(live, _rows(ga, src), 0.0)
    gbs = jnp.where(live, _rows(gb, src), 0.0)
    ys = _grouped_experts(layer, ea, eb, valid, fresh, xs, gas, gbs, w_gate, w_up, w_down)
    return _rows(ys, pos)


def _tail_body(alpha, h1_ref, y_ref, g_ref, b_ref, of_ref, ox_ref):
    out = _ln(alpha * h1_ref[...] + y_ref[...].astype(F32), g_ref[...], b_ref[...])
    of_ref[...] = out

    @pl.when(pl.program_id(0) == 0)
    def _():
        ox_ref[...] = out


def _tail(alpha, h1, y, g, b):
    n, d = h1.shape
    row = pl.BlockSpec((TM, d), lambda i: (i, 0))
    const = pl.BlockSpec((1, d), lambda i: (0, 0))
    return pl.pallas_call(
        functools.partial(_tail_body, alpha), grid=(n // TM,),
        in_specs=[row, row, const, const],
        out_specs=[pl.BlockSpec((TM, d), lambda i: (jnp.maximum(i - 1, 0), 0)),
                   pl.BlockSpec((TM, d), lambda i: (0, 0))],
        out_shape=[jax.ShapeDtypeStruct((n - TM, d), F32), jax.ShapeDtypeStruct((TM, d), F32)],
        compiler_params=_cparams("arbitrary"), name="final_norm",
    )(h1, y, g, b)


def kernel(x_prompt, x_sample, cache_k, cache_v, state_pool, meta_tokens, ln_in_g, ln_in_b, w_in, lambda_qk,
           subln_w, w_pool, pool_scale, w_out, ln1_g, ln1_b, router_w, router_b, w_gate, w_up, w_down,
           ln2_g, ln2_b):
    bsz, seq, d = x_prompt.shape
    n_dec, t_dec, _ = x_sample.shape
    depth = w_in.shape[0]
    past = cache_k.shape[2]
    n_heads = cache_k.shape[3]
    aw = n_heads * HEAD_W
    alpha = (2 * depth) ** 0.25
    n_sample = n_dec * t_dec
    assert seq % TM == 0 and n_sample + N_META <= TM and t_dec == N_META and n_sample % POOL_HALO == 0
    assert past % CHUNK == 0 and t_dec <= CHUNK

    x_extra = jnp.concatenate([x_sample.reshape(n_sample, d), meta_tokens.astype(F32),
                               jnp.zeros((TM - n_sample - N_META, d), F32)], axis=0)
    x_frames = x_prompt.reshape(bsz * seq, d)
    row = lambda v: v.reshape(1, -1).astype(F32)
    slopes = (2.0 ** (-8.0 * jnp.arange(1, n_heads + 1, dtype=F32) / n_heads)).astype(F32)
    bias, diag = _alibi_tables(slopes, seq)
    rw_hi = _bf16_part(router_w.astype(F32))
    rw_lo = router_w.astype(F32) - rw_hi
    r1 = jnp.concatenate([rw_hi, rw_lo, jnp.zeros((d, HEAD_W - 2 * N_EXPERTS), F32)], axis=1).astype(BF16)
    rb = router_b.astype(F32)

    k_out = v_out = None
    h1 = y = None
    pools_p, pools_s, ks_s, vs_s = [], [], [], []
    for l in range(depth):
        lam_init = 0.8 - 0.6 * math.exp(-0.3 * l)
        lq = lambda_qk[l].astype(F32)
        lam = (jnp.exp(jnp.sum(lq[0] * lq[1])) - jnp.exp(jnp.sum(lq[2] * lq[3])) + lam_init).reshape(1)
        sw = row(subln_w[l]) * (1.0 - lam_init)
        w_in_l = w_in[l].astype(BF16)
        if l == 0:
            k1_outs = _k1_first(x_extra, x_frames, row(ln_in_g), row(ln_in_b), w_in_l, depth, bsz, seq)
        else:
            k1_outs = _k1_next(l, alpha, h1, y, row(ln2_g[l - 1]), row(ln2_b[l - 1]), w_in_l, k_out, v_out, seq)
        h, q, u, kb, vb, k_out, v_out, kx, vx = k1_outs
        meta_shape = (bsz, N_META, n_heads, HEAD_W)
        meta_k = jnp.broadcast_to(kx[n_sample:n_sample + N_META].reshape(1, *meta_shape[1:]), meta_shape)
        meta_v = jnp.broadcast_to(vx[n_sample:n_sample + N_META].reshape(1, *meta_shape[1:]), meta_shape)
        k_out = k_out.at[l, :, :N_META].set(meta_k)
        v_out = v_out.at[l, :, :N_META].set(meta_v)
        a = _prompt_attention(lam, q, kb, vb, kx, vx, bias, diag, sw, bsz, seq, n_sample)
        wp = w_pool[l].astype(BF16)
        ps = row(pool_scale[l])
        wo = w_out[l].astype(BF16)
        g1, b1 = row(ln1_g[l]), row(ln1_b[l])
        *x_outs, npool = _extras(l, alpha, slopes, lam, rb, q, kx, vx, u, cache_k, cache_v, state_pool, sw, wp, ps, h, wo,
                                 g1, b1, r1, n_dec, t_dec)
        h1, cls, ga, gb = _frames_mix(alpha, rb, a, u, wp, ps, h, wo, g1, b1, r1, x_outs, seq, n_sample)
        y = _moe(l, h1, cls.reshape(-1), ga.reshape(-1), gb.reshape(-1), w_gate, w_up, w_down)
        keep = state_pool.shape[2]
        last = (TM + (jnp.arange(bsz, dtype=I32)[:, None] + 1) * seq - keep + jnp.arange(keep, dtype=I32)[None, :])
        pools_p.append(_rows(u, last.reshape(-1)).reshape(bsz, keep, -1))
        pools_s.append(npool[:n_dec])
        ks_s.append(kx[:n_sample].reshape(n_dec, t_dec, n_heads, HEAD_W))
        vs_s.append(vx[:n_sample].reshape(n_dec, t_dec, n_heads, HEAD_W))

    y_frames, y_extra = _tail(alpha, h1, y, row(ln2_g[depth - 1]), row(ln2_b[depth - 1]))
    return (y_frames.reshape(bsz, seq, d), y_extra[:n_sample].reshape(n_dec, t_dec, d), k_out, v_out,
            jnp.stack(pools_p), jnp.stack(ks_s), jnp.stack(vs_s), jnp.stack(pools_s))
```

```python
import functools
import math

import jax
import jax.numpy as jnp
from jax import lax
from jax.experimental import pallas as pl
from jax.experimental.pallas import tpu as pltpu

F32, BF16, I32 = jnp.float32, jnp.bfloat16, jnp.int32

CHUNK = 64
N_META = 16
HEAD_DIM = 64
HEAD_W = 2 * HEAD_DIM
POOL_WINDOWS = (2, 4, 8, 16)
POOL_HALO = 16
N_EXPERTS = 16
N_GROUPS = 4
EPG = N_EXPERTS // N_GROUPS
PAIRS = ((0, 1), (0, 2), (0, 3), (1, 2), (1, 3), (2, 3))
N_CLASSES = N_GROUPS * len(PAIRS)
LN_EPS = 1e-5
NEG = -1e30
LOG2E = 1.4426950408889634
Q_SCALE = HEAD_DIM ** -0.5 * LOG2E

TM = 512
TMOE = 256
MOE_CHUNKS = 4
ATT_T = 256
ATT_PAD = 128
VMEM_LIMIT = 56 * 1024 * 1024


def _cparams(*sem):
    return pltpu.CompilerParams(dimension_semantics=sem, vmem_limit_bytes=VMEM_LIMIT)


def _ln(x, g, b):
    mu = jnp.mean(x, axis=-1, keepdims=True)
    xc = x - mu
    var = jnp.mean(xc * xc, axis=-1, keepdims=True)
    return xc * lax.rsqrt(var + LN_EPS) * g + b


def _dot(a, b):
    return jnp.dot(a, b, preferred_element_type=F32)


def _dot_nt(a, b):
    return lax.dot_general(a, b, (((1,), (1,)), ((), ())), preferred_element_type=F32)


def _k1_project(norm, w_ref, h_ref, q_ref, u_ref, kb_ref, vb_ref, k_ref, v_ref, kx_ref, vx_ref, is_extras):
    aw = kx_ref.shape[-1]
    half = TM // 2
    for r0 in (0, half):
        rows = slice(r0, r0 + half)
        hb = norm(rows).astype(BF16)
        h_ref[rows, :] = hb
        q_ref[rows, :] = (_dot(hb, w_ref[:, 0:aw]) * Q_SCALE).astype(BF16)
        k = _dot(hb, w_ref[:, aw:2 * aw])
        v = _dot(hb, w_ref[:, 2 * aw:3 * aw])
        u_ref[rows, :] = _dot(hb, w_ref[:, 3 * aw:])
        kb_ref[rows, :] = k.astype(BF16)
        vb_ref[rows, :] = v.astype(BF16)
        k_ref[rows] = k.reshape((half,) + k_ref.shape[1:])
        v_ref[rows] = v.reshape((half,) + v_ref.shape[1:])
        if is_extras:
            kx_ref[rows, :] = k
            vx_ref[rows, :] = v


def _k1_first_body(xe_ref, x_ref, g_ref, b_ref, w_ref, h_ref, q_ref, u_ref, kb_ref, vb_ref, k_ref, v_ref, kx_ref,
                   vx_ref):
    outs = (w_ref, h_ref, q_ref, u_ref, kb_ref, vb_ref, k_ref, v_ref, kx_ref, vx_ref)

    @pl.when(pl.program_id(0) == 0)
    def _():
        _k1_project(lambda rows: _ln(xe_ref[rows, :], g_ref[...], b_ref[...]), *outs, True)

    @pl.when(pl.program_id(0) > 0)
    def _():
        _k1_project(lambda rows: _ln(x_ref[rows, :], g_ref[...], b_ref[...]), *outs, False)


def _k1_next_body(alpha, h1_ref, y_ref, g_ref, b_ref, w_ref, ko_ref, vo_ref,
                  h_ref, q_ref, u_ref, kb_ref, vb_ref, k_ref, v_ref, kx_ref, vx_ref):
    del ko_ref, vo_ref
    outs = (w_ref, h_ref, q_ref, u_ref, kb_ref, vb_ref, k_ref, v_ref, kx_ref, vx_ref)

    def norm(rows):
        return _ln(alpha * h1_ref[rows, :] + y_ref[rows, :].astype(F32), g_ref[...], b_ref[...])

    @pl.when(pl.program_id(0) == 0)
    def _():
        _k1_project(norm, *outs, True)

    @pl.when(pl.program_id(0) > 0)
    def _():
        _k1_project(norm, *outs, False)


def _k1_specs(layer, n_tiles, tiles_per_b, d, aw, pw):
    def frames_tile(i):
        return jnp.maximum(i - 1, 0)

    def kv_map(i):
        t = frames_tile(i)
        return (layer, t // tiles_per_b, pl.multiple_of(N_META + (t % tiles_per_b) * TM, 16), 0, 0)

    kv_spec = pl.BlockSpec((None, None, pl.Element(TM), pl.Element(aw // HEAD_W), pl.Element(HEAD_W)), kv_map)
    row = lambda w: pl.BlockSpec((TM, w), lambda i: (i, 0))
    const = lambda r, w: pl.BlockSpec((r, w), lambda i: (0, 0))
    out_specs = [row(d), row(aw), row(pw), row(aw), row(aw), kv_spec, kv_spec, const(TM, aw), const(TM, aw)]
    return row, const, frames_tile, out_specs


def _k1_out_shapes(n, d, aw, pw, depth, b, l):
    return [
        jax.ShapeDtypeStruct((n, d), BF16),
        jax.ShapeDtypeStruct((n, aw), BF16),
        jax.ShapeDtypeStruct((n, pw), F32),
        jax.ShapeDtypeStruct((n, aw), BF16),
        jax.ShapeDtypeStruct((n, aw), BF16),
        jax.ShapeDtypeStruct((depth, b, l, aw // HEAD_W, HEAD_W), F32),
        jax.ShapeDtypeStruct((depth, b, l, aw // HEAD_W, HEAD_W), F32),
        jax.ShapeDtypeStruct((TM, aw), F32),
        jax.ShapeDtypeStruct((TM, aw), F32),
    ]


def _k1_first(x_extra, x_frames, g, b, w, depth, bsz, seq):
    nf, d = x_frames.shape
    aw = w.shape[1] // 4
    pw = w.shape[1] - 3 * aw
    n = TM + nf
    n_tiles = n // TM
    tpb = seq // TM
    row, const, frames_tile, out_specs = _k1_specs(0, n_tiles, tpb, d, aw, pw)
    in_specs = [const(TM, d), pl.BlockSpec((TM, d), lambda i: (frames_tile(i), 0)),
                const(1, d), const(1, d), const(d, w.shape[1])]
    return pl.pallas_call(
        _k1_first_body, grid=(n_tiles,), in_specs=in_specs, out_specs=out_specs,
        out_shape=_k1_out_shapes(n, d, aw, pw, depth, bsz, N_META + seq),
        compiler_params=_cparams("arbitrary"), name="k1_first",
    )(x_extra, x_frames, g, b, w)


def _k1_next(layer, alpha, h1, y, g, b, w, k_out, v_out, seq):
    n, d = h1.shape
    depth, bsz, l, n_heads, _ = k_out.shape
    aw = n_heads * HEAD_W
    pw = w.shape[1] - 3 * aw
    n_tiles = n // TM
    tpb = seq // TM
    row, const, _, out_specs = _k1_specs(layer, n_tiles, tpb, d, aw, pw)
    any_spec = pl.BlockSpec(memory_space=pl.ANY)
    in_specs = [row(d), row(d), const(1, d), const(1, d), const(d, w.shape[1]), any_spec, any_spec]
    return pl.pallas_call(
        functools.partial(_k1_next_body, alpha), grid=(n_tiles,), in_specs=in_specs, out_specs=out_specs,
        out_shape=_k1_out_shapes(n, d, aw, pw, depth, bsz, l),
        input_output_aliases={5: 5, 6: 6},
        compiler_params=_cparams("arbitrary"), name="k1_next",
    )(h1, y, g, b, w, k_out, v_out)


def _bf16_part(x):
    bits = lax.bitcast_convert_type(x, jnp.uint32) & jnp.uint32(0xFFFF0000)
    return lax.bitcast_convert_type(bits, F32)


def _split3(c):
    hi = _bf16_part(c)
    r = c - hi
    mid = _bf16_part(r)
    return hi, mid, r - mid


def _alibi_tables(slopes, seq):
    pad = ATT_PAD - N_META
    pos = jnp.concatenate([jnp.zeros((pad,), F32), jnp.arange(N_META + seq, dtype=F32)])
    live = jnp.arange(ATT_PAD + seq) >= pad
    c = jnp.where(live[None, :], (slopes * LOG2E)[:, None] * pos[None, :], NEG)
    bias = jnp.pad(jnp.stack(_split3(c), axis=-1), ((0, 0), (0, 0), (0, HEAD_W - 3))).astype(BF16)
    rr = lax.broadcasted_iota(I32, (ATT_T, ATT_T), 0)
    cc = lax.broadcasted_iota(I32, (ATT_T, ATT_T), 1)
    ahead = jnp.maximum(cc - rr, 0).astype(F32)
    diag = jnp.where(((cc // CHUNK) <= (rr // CHUNK))[None], (-2.0 * LOG2E * slopes)[:, None, None] * ahead[None], NEG)
    return bias, diag


def _attn_body(lam_ref, q_ref, kb_ref, vb_ref, kx_ref, vx_ref, bias_ref, diag_ref, sw_ref, o_ref, ka, vm, qa1, qa2,
               sc):
    t = ATT_T
    lam = lam_ref[0]
    s_len = q_ref.shape[0]
    n_sub = s_len // t
    kp = ATT_PAD
    pad = kp - N_META
    lane = lax.broadcasted_iota(I32, (1, HEAD_W), 1)
    first_half = (lane < HEAD_DIM).astype(BF16)
    ones = jnp.broadcast_to((lane < 3).astype(BF16), (s_len, HEAD_W))

    ka[0:pad, 0:HEAD_W] = jnp.zeros((pad, HEAD_W), BF16)
    ka[pad:kp, 0:HEAD_W] = kx_ref[...].astype(BF16)
    ka[kp:, 0:HEAD_W] = kb_ref[...]
    ka[:, HEAD_W:] = bias_ref[...]
    vm[0:pad, :] = jnp.zeros((pad, HEAD_W), BF16)
    vm[pad:kp, :] = vx_ref[...].astype(BF16)
    qb = q_ref[...]
    qa1[:, 0:HEAD_W] = qb * first_half
    qa2[:, 0:HEAD_W] = qb * (1 - first_half)
    qa1[:, HEAD_W:] = ones
    qa2[:, HEAD_W:] = ones

    def tiles(sb):
        return [(0, kp)] + [(kp + j * t, t) for j in range(sb + 1)]

    def fold(x, op):
        return x if x.shape[1] == HEAD_W else op(x[:, :HEAD_W], x[:, HEAD_W:])

    def scores(sb):
        row_max = []
        for mp, qa in enumerate((qa1, qa2)):
            q = qa[sb * t:(sb + 1) * t, :]
            macc = None
            tl = tiles(sb)
            for i, (c0, w) in enumerate(tl):
                s = _dot_nt(q, ka[c0:c0 + w, :])
                if i == len(tl) - 1:
                    s = s + diag_ref[...]
                sc[sb % 2, mp, :, c0:c0 + w] = s
                f = fold(s, jnp.maximum)
                macc = f if macc is None else jnp.maximum(macc, f)
            row_max.append(jnp.max(macc, axis=-1, keepdims=True))
        return row_max

    nxt = scores(0)
    for sb in range(n_sub):
        q0 = sb * t
        row_max = nxt
        if sb + 1 < n_sub:
            nxt = scores(sb + 1)
        ls = []
        for mp in range(2):
            mb = jnp.broadcast_to(row_max[mp], (t, HEAD_W))
            lacc = None
            for c0, w in tiles(sb):
                s = sc[sb % 2, mp, :, c0:c0 + w]
                p = jnp.exp2(s - (mb if w == HEAD_W else jnp.concatenate([mb, mb], axis=-1)))
                sc[sb % 2, mp, :, c0:c0 + w] = p
                f = fold(p, jnp.add)
                lacc = f if lacc is None else lacc + f
            ls.append(jnp.sum(lacc, axis=-1, keepdims=True))
        fac = lam * ls[0] / ls[1]
        acc = None
        for c0, w in tiles(sb):
            wt = (sc[sb % 2, 0, :, c0:c0 + w] - sc[sb % 2, 1, :, c0:c0 + w] * fac).astype(BF16)
            d = _dot(wt, vm[...] if c0 == 0 else vb_ref[c0 - kp:c0 - kp + w, :])
            acc = d if acc is None else acc + d
        o = acc / ls[0]
        o = o * lax.rsqrt(jnp.mean(o * o, axis=-1, keepdims=True) + LN_EPS)
        o_ref[q0:q0 + t, :] = (o * sw_ref[...]).astype(BF16)


def _prompt_attention(lam, q, kb, vb, kx, vx, bias, diag, sw, bsz, seq, meta_row):
    n, aw = q.shape
    n_heads = aw // HEAD_W
    key_rows = ATT_PAD + seq
    smem = pl.BlockSpec(memory_space=pltpu.SMEM)
    frames = pl.BlockSpec((pl.Element(seq), pl.Element(HEAD_W)),
                          lambda b, h: (pl.multiple_of(TM + b * seq, TM), pl.multiple_of(h * HEAD_W, HEAD_W)))
    meta = pl.BlockSpec((N_META, HEAD_W), lambda b, h: (meta_row // N_META, h))
    return pl.pallas_call(
        _attn_body, grid=(bsz, n_heads),
        in_specs=[smem, frames, frames, frames, meta, meta,
                  pl.BlockSpec((None, key_rows, HEAD_W), lambda b, h: (h, 0, 0)),
                  pl.BlockSpec((None, ATT_T, ATT_T), lambda b, h: (h, 0, 0)),
                  pl.BlockSpec((1, HEAD_W), lambda b, h: (0, 0))],
        out_specs=frames,
        out_shape=jax.ShapeDtypeStruct((n, aw), BF16),
        scratch_shapes=[pltpu.VMEM((key_rows, 2 * HEAD_W), BF16), pltpu.VMEM((ATT_PAD, HEAD_W), BF16),
                        pltpu.VMEM((seq, 2 * HEAD_W), BF16), pltpu.VMEM((seq, 2 * HEAD_W), BF16),
                        pltpu.VMEM((2, 2, ATT_T, key_rows), F32)],
        compiler_params=_cparams("arbitrary", "arbitrary"), name="prompt_attention",
    )(lam, q, kb, vb, kx, vx, bias, diag, sw)


def _pool_mix(ext_ref, n_rows, counts, wp_ref, ps_ref, row0=0):
    gw = wp_ref.shape[-1]
    outs = []
    base = POOL_HALO + row0
    for g, w in enumerate(POOL_WINDOWS):
        cols = slice(g * gw, (g + 1) * gw)
        tok = ext_ref[base:base + n_rows, cols]
        acc = tok
        for i in range(1, w):
            acc = acc + ext_ref[base - i:base - i + n_rows, cols]
        if counts is None:
            d = acc * (1.0 / w) - tok
        else:
            d = acc / jnp.minimum(counts, float(w)) - tok
        outs.append(_dot(d.astype(BF16), wp_ref[g]))
    return jnp.concatenate(outs, axis=-1) * ps_ref[...]


def _route(h1, r1_ref, rb_ref, cls_ref, ga_ref, gb_ref):
    hi = h1.astype(BF16)
    lo = (h1 - hi.astype(F32)).astype(BF16)
    at = _dot(hi, r1_ref[...]).T
    bt = _dot(lo, r1_ref[...]).T
    logit = at[0:N_EXPERTS] + at[N_EXPERTS:2 * N_EXPERTS] + bt[0:N_EXPERTS]
    lg = [logit[e:e + 1, :] for e in range(N_EXPERTS)]
    m = functools.reduce(jnp.maximum, lg)
    ex = [jnp.exp(x - m) for x in lg]
    z = functools.reduce(jnp.add, ex)
    prob = [x / z for x in ex]
    sel = [prob[e] + rb_ref[e] for e in range(N_EXPERTS)]

    def top2_sum(v):
        best = None
        for a, b in PAIRS:
            s = v[a] + v[b]
            best = s if best is None else jnp.maximum(best, s)
        return best

    score = [top2_sum(sel[g * EPG:(g + 1) * EPG]) for g in range(N_GROUPS)]
    gi = jnp.zeros_like(score[0], dtype=I32)
    best = score[0]
    for g in range(1, N_GROUPS):
        better = score[g] > best
        gi = jnp.where(better, g, gi)
        best = jnp.where(better, score[g], best)

    def pick(vals, j):
        out = vals[j]
        for g in range(1, N_GROUPS):
            out = jnp.where(gi == g, vals[g * EPG + j], out)
        return out

    sg = [pick(sel, j) for j in range(EPG)]
    pg = [pick(prob, j) for j in range(EPG)]

    def argmax_first(v, skip=None):
        bi = jnp.zeros_like(gi)
        bv = None
        for j in range(EPG):
            x = v[j] if skip is None else jnp.where(skip == j, -jnp.inf, v[j])
            if bv is None:
                bv = x
            else:
                better = x > bv
                bi = jnp.where(better, j, bi)
                bv = jnp.where(better, x, bv)
        return bi

    e1 = argmax_first(sg)
    e2 = argmax_first(sg, skip=e1)

    def at_idx(v, idx):
        out = v[0]
        for j in range(1, EPG):
            out = jnp.where(idx == j, v[j], out)
        return out

    g1 = at_idx(pg, e1)
    g2 = at_idx(pg, e2)
    tot = g1 + g2
    g1 = g1 / tot
    g2 = g2 / tot
    swap = e2 < e1
    ea = jnp.where(swap, e2, e1)
    eb = jnp.where(swap, e1, e2)
    pair = jnp.zeros_like(gi)
    for p, (a, b) in enumerate(PAIRS):
        pair = jnp.where((ea == a) & (eb == b), p, pair)
    cls_ref[...] = gi * len(PAIRS) + pair
    ga_ref[...] = jnp.where(swap, g2, g1)
    gb_ref[...] = jnp.where(swap, g1, g2)


def _post(alpha, mix, h, wo_ref, g_ref, b_ref, r1_ref, rb_ref, h1_ref, cls_ref, ga_ref, gb_ref):
    h1 = _ln(alpha * h.astype(F32) + _dot(mix, wo_ref[...]), g_ref[...], b_ref[...])
    h1_ref[...] = h1
    _route(h1, r1_ref, rb_ref, cls_ref, ga_ref, gb_ref)


def _post_out_shapes(n, d):
    nt = n // TM
    return [jax.ShapeDtypeStruct((n, d), F32), jax.ShapeDtypeStruct((nt, 1, TM), I32),
            jax.ShapeDtypeStruct((nt, 1, TM), F32), jax.ShapeDtypeStruct((nt, 1, TM), F32)]


def _kx_body(alpha, n_dec, t_dec, slope_ref, lam_ref, rb_ref, q_ref, kx_ref, vx_ref, ux_ref, ck_ref, cv_ref,
             sp_ref, sw_ref, wp_ref, ps_ref, h_ref, wo_ref, g_ref, b_ref, r1_ref,
             h1_ref, cls_ref, ga_ref, gb_ref, npool_ref, mix, ext):
    s = pl.program_id(0)
    past = ck_ref.shape[0]
    aw = q_ref.shape[1]
    n_heads = aw // HEAD_W
    is_meta = s == n_dec
    lam = lam_ref[0]
    r0 = pl.multiple_of(s * t_dec, t_dec)

    @pl.when(s == 0)
    def _():
        mix[...] = jnp.zeros_like(mix)

    q = q_ref[pl.ds(r0, t_dec), :].astype(F32)
    kn = kx_ref[pl.ds(r0, t_dec), :]
    vn = vx_ref[pl.ds(r0, t_dec), :]
    rq = lax.broadcasted_iota(I32, (t_dec, past), 0)
    cp = lax.broadcasted_iota(I32, (t_dec, past), 1)
    dist_p = (past + rq - cp).astype(F32)
    rn = lax.broadcasted_iota(I32, (t_dec, t_dec), 0)
    cn = lax.broadcasted_iota(I32, (t_dec, t_dec), 1)
    dist_n = jnp.abs(rn - cn).astype(F32)
    heads = []
    for hd in range(n_heads):
        slope = slope_ref[hd] * LOG2E
        c0 = hd * HEAD_W
        ckh = ck_ref[:, hd, :].astype(BF16)
        cvh = cv_ref[:, hd, :].astype(BF16)
        maps = []
        for half in range(2):
            lo_c = c0 + half * HEAD_DIM
            qh = q[:, lo_c:lo_c + HEAD_DIM].astype(BF16)
            sp = _dot_nt(qh, ckh[:, half * HEAD_DIM:(half + 1) * HEAD_DIM]) - slope * dist_p
            sp = jnp.where(is_meta, NEG, sp)
            sn = _dot_nt(qh, kn[:, lo_c:lo_c + HEAD_DIM].astype(BF16)) - slope * dist_n
            m = jnp.maximum(jnp.max(sp, axis=-1, keepdims=True), jnp.max(sn, axis=-1, keepdims=True))
            pp = jnp.exp2(sp - m)
            pn = jnp.exp2(sn - m)
            z = jnp.sum(pp, axis=-1, keepdims=True) + jnp.sum(pn, axis=-1, keepdims=True)
            maps.append((pp / z, pn / z))
        wp_ = (maps[0][0] - lam * maps[1][0]).astype(BF16)
        wn_ = (maps[0][1] - lam * maps[1][1]).astype(BF16)
        o = _dot(wp_, cvh) + _dot(wn_, vn[:, c0:c0 + HEAD_W].astype(BF16))
        o = o * lax.rsqrt(jnp.mean(o * o, axis=-1, keepdims=True) + LN_EPS)
        heads.append(o * sw_ref[...])
    attn = jnp.concatenate(heads, axis=-1)

    hist = jnp.where(is_meta, 0.0, sp_ref[...])
    u = ux_ref[pl.ds(r0, t_dec), :]
    ext[POOL_HALO - sp_ref.shape[0]:POOL_HALO, :] = hist
    ext[POOL_HALO:POOL_HALO + t_dec, :] = u
    pos = lax.broadcasted_iota(I32, (t_dec, 1), 0) + jnp.where(is_meta, 0, past)
    pool = _pool_mix(ext, t_dec, (pos + 1).astype(F32), wp_ref, ps_ref)
    mix[pl.ds(r0, t_dec), :] = jnp.concatenate([attn, pool], axis=-1).astype(BF16)
    keep = sp_ref.shape[0]
    npool_ref[...] = ext[POOL_HALO + t_dec - keep:POOL_HALO + t_dec, :]

    @pl.when(s == n_dec)
    def _():
        _post(alpha, mix[...], h_ref[...], wo_ref, g_ref, b_ref, r1_ref, rb_ref,
              h1_ref, cls_ref, ga_ref, gb_ref)


def _extras(layer, alpha, slopes, lam, rb, q, kx, vx, u, cache_k, cache_v, state_pool, sw, wp, ps, h, wo, g, b,
            r1, n_dec, t_dec):
    n, d = h.shape
    aw = q.shape[1]
    pw = u.shape[1]
    past = cache_k.shape[2]
    keep = state_pool.shape[2]
    smem = pl.BlockSpec(memory_space=pltpu.SMEM)
    tile0 = lambda w: pl.BlockSpec((TM, w), lambda s: (0, 0))
    full = lambda *shape: pl.BlockSpec(shape, lambda s: (0,) * len(shape))
    dec = lambda s: jnp.minimum(s, n_dec - 1)
    cache_spec = pl.BlockSpec((None, None, past, aw // HEAD_W, HEAD_W), lambda s: (layer, dec(s), 0, 0, 0))
    small = pl.BlockSpec((None, 1, TM), lambda s: (0, 0, 0))
    out_shapes = _post_out_shapes(n, d) + [jax.ShapeDtypeStruct((n_dec + 1, keep, pw), F32)]
    out_specs = [tile0(d), small, small, small, pl.BlockSpec((None, keep, pw), lambda s: (s, 0, 0))]
    return pl.pallas_call(
        functools.partial(_kx_body, alpha, n_dec, t_dec), grid=(n_dec + 1,),
        in_specs=[smem, smem, smem, tile0(aw), full(TM, aw), full(TM, aw), tile0(pw), cache_spec, cache_spec,
                  pl.BlockSpec((None, None, keep, pw), lambda s: (layer, dec(s), 0, 0)),
                  full(1, HEAD_W), full(*wp.shape), full(1, pw), tile0(d), full(*wo.shape), full(1, d), full(1, d),
                  full(*r1.shape)],
        out_specs=out_specs, out_shape=out_shapes,
        scratch_shapes=[pltpu.VMEM((TM, aw + pw), BF16), pltpu.VMEM((POOL_HALO + t_dec, pw), F32)],
        compiler_params=_cparams("arbitrary"), name="extras_mix",
    )(slopes, lam, rb, q, kx, vx, u, cache_k, cache_v, state_pool, sw, wp, ps, h, wo, g, b, r1)


def _k3_body(alpha, tiles_per_b, rb_ref, a_ref, u_ref, uprev_ref, umeta_ref, wp_ref, ps_ref, h_ref, wo_ref, g_ref,
             b_ref, r1_ref, h1_in, cls_in, ga_in, gb_in, h1_ref, cls_ref, ga_ref, gb_ref, ext):
    del h1_in, cls_in, ga_in, gb_in
    first = (pl.program_id(0) % tiles_per_b) == 0
    ext[0:POOL_HALO, :] = jnp.where(first, umeta_ref[...], uprev_ref[...])
    ext[POOL_HALO:, :] = u_ref[...]
    half = TM // 2
    pre = []
    for r0 in (0, half):
        rows = slice(r0, r0 + half)
        pool = _pool_mix(ext, half, None, wp_ref, ps_ref, row0=r0)
        mix = jnp.concatenate([a_ref[rows, :], pool.astype(BF16)], axis=-1)
        pre.append(alpha * h_ref[rows, :].astype(F32) + _dot(mix, wo_ref[...]))
    for r0, z in zip((0, half), pre):
        rows = slice(r0, r0 + half)
        h1 = _ln(z, g_ref[...], b_ref[...])
        h1_ref[rows, :] = h1
        _route(h1, r1_ref, rb_ref, cls_ref.at[:, rows], ga_ref.at[:, rows], gb_ref.at[:, rows])


def _frames_mix(alpha, rb, a, u, wp, ps, h, wo, g, b, r1, prev_outs, seq, meta_row):
    n, d = h.shape
    aw, pw = a.shape[1], u.shape[1]
    n_tiles = n // TM - 1
    tpb = seq // TM
    smem = pl.BlockSpec(memory_space=pltpu.SMEM)
    row = lambda w: pl.BlockSpec((TM, w), lambda i: (i + 1, 0))
    full = lambda *shape: pl.BlockSpec(shape, lambda i: (0,) * len(shape))
    hb = TM // POOL_HALO
    small = pl.BlockSpec((None, 1, TM), lambda i: (i + 1, 0, 0))
    any_spec = pl.BlockSpec(memory_space=pl.ANY)
    return pl.pallas_call(
        functools.partial(_k3_body, alpha, tpb), grid=(n_tiles,),
        in_specs=[smem, row(aw), row(pw),
                  pl.BlockSpec((POOL_HALO, pw), lambda i: ((i + 1) * hb - 1, 0)),
                  pl.BlockSpec((POOL_HALO, pw), lambda i: (meta_row // POOL_HALO, 0)),
                  full(*wp.shape), full(1, pw), row(d), full(*wo.shape), full(1, d), full(1, d), full(*r1.shape)]
                 + [any_spec] * 4,
        out_specs=[row(d), small, small, small],
        out_shape=_post_out_shapes(n, d),
        input_output_aliases={12: 0, 13: 1, 14: 2, 15: 3},
        scratch_shapes=[pltpu.VMEM((POOL_HALO + TM, pw), F32)],
        compiler_params=_cparams("arbitrary"), name="frames_mix",
    )(rb, a, u, u, u, wp, ps, h, wo, g, b, r1, *prev_outs)


def _moe_body(ea_ref, eb_ref, valid_ref, x_ref, ga_ref, gb_ref, wga_f, wua_f, wda_f, wgb_f, wub_f, wdb_f, ys_in,
              y_ref, wga, wua, wda, wgb, wub, wdb):
    del ys_in
    t = pl.program_id(0)
    prev = jnp.maximum(t - 1, 0)
    fresh = (t == 0) | (ea_ref[t] != ea_ref[prev]) | (eb_ref[t] != eb_ref[prev])

    @pl.when(fresh)
    def _():
        for src, dst in ((wga_f, wga), (wua_f, wua), (wda_f, wda), (wgb_f, wgb), (wub_f, wub), (wdb_f, wdb)):
            dst[...] = src[...].astype(BF16)

    @pl.when(valid_ref[pl.program_id(0)] > 0)
    def _():
        x = x_ref[...].astype(BF16)
        rows = x.shape[0]

        def lane_bcast(g_ref):
            return jnp.broadcast_to(g_ref[...], (HEAD_W, rows)).T

        def expert(wg, wu, wd, gate):
            gt = _dot(x, wg[...])
            hid = gt * (1.0 / (1.0 + jnp.exp(-gt))) * _dot(x, wu[...])
            reps = hid.shape[1] // HEAD_W
            hid = hid * jnp.concatenate([gate] * reps, axis=-1)
            return _dot(hid.astype(BF16), wd[...])

        y = expert(wga, wua, wda, lane_bcast(ga_ref))
        y_ref[...] = (y + expert(wgb, wub, wdb, lane_bcast(gb_ref))).astype(y_ref.dtype)

    @pl.when(valid_ref[pl.program_id(0)] == 0)
    def _():
        y_ref[...] = jnp.zeros_like(y_ref)


def _grouped_experts(layer, tile0, n_pad, ea, eb, valid, xs, gas, gbs, w_gate, w_up, w_down, ys):
    rows_here, d = xs.shape
    n_tiles = rows_here // TMOE
    de = w_gate.shape[3]
    wa = lambda r, c: pl.BlockSpec((None, None, r, c), lambda t, ea, eb, valid: (layer, ea[t], 0, 0))
    wb = lambda r, c: pl.BlockSpec((None, None, r, c), lambda t, ea, eb, valid: (layer, eb[t], 0, 0))
    gate_spec = pl.BlockSpec((None, 1, TMOE), lambda t, ea, eb, valid: (t, 0, 0))
    in_rows = pl.BlockSpec((TMOE, d), lambda t, ea, eb, valid: (t, 0))
    out_rows = pl.BlockSpec((TMOE, d), lambda t, ea, eb, valid: (t + tile0, 0))
    if ys is None:
        ys = jnp.zeros((8, HEAD_W), BF16)
        aliases = {}
    else:
        aliases = {12: 0}
    grid_spec = pltpu.PrefetchScalarGridSpec(
        num_scalar_prefetch=3, grid=(n_tiles,),
        in_specs=[in_rows, gate_spec, gate_spec, wa(d, de), wa(d, de), wa(de, d), wb(d, de), wb(d, de), wb(de, d),
                  pl.BlockSpec(memory_space=pl.ANY)],
        out_specs=out_rows,
        scratch_shapes=[pltpu.VMEM((d, de), BF16), pltpu.VMEM((d, de), BF16), pltpu.VMEM((de, d), BF16)] * 2,
    )
    return pl.pallas_call(
        _moe_body, grid_spec=grid_spec, out_shape=jax.ShapeDtypeStruct((n_pad, d), BF16),
        input_output_aliases=aliases,
        compiler_params=_cparams("arbitrary"), name="grouped_experts",
    )(ea, eb, valid, xs, gas.reshape(n_tiles, 1, TMOE), gbs.reshape(n_tiles, 1, TMOE),
      w_gate, w_up, w_down, w_gate, w_up, w_down, ys)


def _rows(x, idx):
    return x.at[idx].get(mode="promise_in_bounds")


def _dispatch_plan(cls):
    n = cls.shape[0]
    n_pad = n + N_CLASSES * TMOE
    classes = jnp.arange(N_CLASSES, dtype=I32)
    onehot = (cls[:, None] == classes[None, :]).astype(I32)
    rank = jnp.sum(jnp.cumsum(onehot, axis=0) * onehot, axis=1) - 1
    counts = jnp.sum(onehot, axis=0)
    padded = (counts + TMOE - 1) // TMOE * TMOE
    ends = jnp.cumsum(padded)
    starts = ends - padded
    first = jnp.cumsum(counts) - counts
    pos = jnp.sum(onehot * starts[None, :], axis=1) + rank
    shift = max(n - 1, 1).bit_length()
    order = jnp.sort(cls * (1 << shift) + jnp.arange(n, dtype=I32)) & ((1 << shift) - 1)
    row = jnp.arange(n_pad, dtype=I32)
    row_cls = jnp.minimum(jnp.sum((ends[None, :] <= row[:, None]).astype(I32), axis=1), N_CLASSES - 1)
    row_hot = (row_cls[:, None] == classes[None, :]).astype(I32)
    within = row - jnp.sum(row_hot * starts[None, :], axis=1)
    live = within < jnp.sum(row_hot * counts[None, :], axis=1)
    src = _rows(order, jnp.clip(jnp.sum(row_hot * first[None, :], axis=1) + within, 0, n - 1))
    src = jnp.where(live, src, row % n)
    tile_cls = row_cls[::TMOE]
    valid = (row[::TMOE] < ends[-1]).astype(I32)
    pa = jnp.array([p[0] for p in PAIRS], I32)
    pb = jnp.array([p[1] for p in PAIRS], I32)
    grp = tile_cls // len(PAIRS)
    pair_hot = ((tile_cls % len(PAIRS))[:, None] == jnp.arange(len(PAIRS), dtype=I32)[None, :]).astype(I32)
    ea = grp * EPG + jnp.sum(pair_hot * pa[None, :], axis=1)
    eb = grp * EPG + jnp.sum(pair_hot * pb[None, :], axis=1)
    return pos.astype(I32), src, live, ea.astype(I32), eb.astype(I32), valid


def _moe(layer, h1, cls, ga, gb, w_gate, w_up, w_down):
    pos, src, live, ea, eb, valid = _dispatch_plan(cls)
    gas = jnp.where(live, _rows(ga, src), 0.0)
    gbs = jnp.where(live, _rows(gb, src), 0.0)
    n_pad = src.shape[0]
    n_tiles = n_pad // TMOE
    cuts = [n_tiles * i // MOE_CHUNKS for i in range(MOE_CHUNKS + 1)]
    ys = None
    for t0, t1 in zip(cuts[:-1], cuts[1:]):
        r = slice(t0 * TMOE, t1 * TMOE)
        ys = _grouped_experts(layer, t0, n_pad, ea[t0:t1], eb[t0:t1], valid[t0:t1], _rows(h1, src[r]), gas[r], gbs[r],
                              w_gate, w_up, w_down, ys)
    return _rows(ys, pos)


def _tail_body(alpha, h1_ref, y_ref, g_ref, b_ref, of_ref, ox_ref):
    out = _ln(alpha * h1_ref[...] + y_ref[...].astype(F32), g_ref[...], b_ref[...])
    of_ref[...] = out

    @pl.when(pl.program_id(0) == 0)
    def _():
        ox_ref[...] = out


def _tail(alpha, h1, y, g, b):
    n, d = h1.shape
    row = pl.BlockSpec((TM, d), lambda i: (i, 0))
    const = pl.BlockSpec((1, d), lambda i: (0, 0))
    return pl.pallas_call(
        functools.partial(_tail_body, alpha), grid=(n // TM,),
        in_specs=[row, row, const, const],
        out_specs=[pl.BlockSpec((TM, d), lambda i: (jnp.maximum(i - 1, 0), 0)),
                   pl.BlockSpec((TM, d), lambda i: (0, 0))],
        out_shape=[jax.ShapeDtypeStruct((n - TM, d), F32), jax.ShapeDtypeStruct((TM, d), F32)],
        compiler_params=_cparams("arbitrary"), name="final_norm",
    )(h1, y, g, b)


def kernel(x_prompt, x_sample, cache_k, cache_v, state_pool, meta_tokens, ln_in_g, ln_in_b, w_in, lambda_qk,
           subln_w, w_pool, pool_scale, w_out, ln1_g, ln1_b, router_w, router_b, w_gate, w_up, w_down,
           ln2_g, ln2_b):
    bsz, seq, d = x_prompt.shape
    n_dec, t_dec, _ = x_sample.shape
    depth = w_in.shape[0]
    past = cache_k.shape[2]
    n_heads = cache_k.shape[3]
    aw = n_heads * HEAD_W
    alpha = (2 * depth) ** 0.25
    n_sample = n_dec * t_dec
    assert seq % TM == 0 and n_sample + N_META <= TM and t_dec == N_META and n_sample % POOL_HALO == 0
    assert past % CHUNK == 0 and t_dec <= CHUNK

    x_extra = jnp.concatenate([x_sample.reshape(n_sample, d), meta_tokens.astype(F32),
                               jnp.zeros((TM - n_sample - N_META, d), F32)], axis=0)
    x_frames = x_prompt.reshape(bsz * seq, d)
    row = lambda v: v.reshape(1, -1).astype(F32)
    slopes = (2.0 ** (-8.0 * jnp.arange(1, n_heads + 1, dtype=F32) / n_heads)).astype(F32)
    bias, diag = _alibi_tables(slopes, seq)
    rw_hi = _bf16_part(router_w.astype(F32))
    rw_lo = router_w.astype(F32) - rw_hi
    r1 = jnp.concatenate([rw_hi, rw_lo, jnp.zeros((d, HEAD_W - 2 * N_EXPERTS), F32)], axis=1).astype(BF16)
    rb = router_b.astype(F32)

    k_out = v_out = None
    h1 = y = None
    pools_p, pools_s, ks_s, vs_s = [], [], [], []
    for l in range(depth):
        lam_init = 0.8 - 0.6 * math.exp(-0.3 * l)
        lq = lambda_qk[l].astype(F32)
        lam = (jnp.exp(jnp.sum(lq[0] * lq[1])) - jnp.exp(jnp.sum(lq[2] * lq[3])) + lam_init).reshape(1)
        sw = row(subln_w[l]) * (1.0 - lam_init)
        w_in_l = w_in[l].astype(BF16)
        if l == 0:
            k1_outs = _k1_first(x_extra, x_frames, row(ln_in_g), row(ln_in_b), w_in_l, depth, bsz, seq)
        else:
            k1_outs = _k1_next(l, alpha, h1, y, row(ln2_g[l - 1]), row(ln2_b[l - 1]), w_in_l, k_out, v_out, seq)
        h, q, u, kb, vb, k_out, v_out, kx, vx = k1_outs
        meta_shape = (bsz, N_META, n_heads, HEAD_W)
        meta_k = jnp.broadcast_to(kx[n_sample:n_sample + N_META].reshape(1, *meta_shape[1:]), meta_shape)
        meta_v = jnp.broadcast_to(vx[n_sample:n_sample + N_META].reshape(1, *meta_shape[1:]), meta_shape)
        k_out = k_out.at[l, :, :N_META].set(meta_k)
        v_out = v_out.at[l, :, :N_META].set(meta_v)
        a = _prompt_attention(lam, q, kb, vb, kx, vx, bias, diag, sw, bsz, seq, n_sample)
        wp = w_pool[l].astype(BF16)
        ps = row(pool_scale[l])
        wo = w_out[l].astype(BF16)
        g1, b1 = row(ln1_g[l]), row(ln1_b[l])
        *x_outs, npool = _extras(l, alpha, slopes, lam, rb, q, kx, vx, u, cache_k, cache_v, state_pool, sw, wp, ps, h, wo,
                                 g1, b1, r1, n_dec, t_dec)
        h1, cls, ga, gb = _frames_mix(alpha, rb, a, u, wp, ps, h, wo, g1, b1, r1, x_outs, seq, n_sample)
        y = _moe(l, h1, cls.reshape(-1), ga.reshape(-1), gb.reshape(-1), w_gate, w_up, w_down)
        keep = state_pool.shape[2]
        last = (TM + (jnp.arange(bsz, dtype=I32)[:, None] + 1) * seq - keep + jnp.arange(keep, dtype=I32)[None, :])
        pools_p.append(_rows(u, last.reshape(-1)).reshape(bsz, keep, -1))
        pools_s.append(npool[:n_dec])
        ks_s.append(kx[:n_sample].reshape(n_dec, t_dec, n_heads, HEAD_W))
        vs_s.append(vx[:n_sample].reshape(n_dec, t_dec, n_heads, HEAD_W))

    y_frames, y_extra = _tail(alpha, h1, y, row(ln2_g[depth - 1]), row(ln2_b[depth - 1]))
    return (y_frames.reshape(bsz, seq, d), y_extra[:n_sample].reshape(n_dec, t_dec, d), k_out, v_out,
            jnp.stack(pools_p), jnp.stack(ks_s), jnp.stack(vs_s), jnp.stack(pools_s))
```

```python
import functools
import math

import jax
import jax.numpy as jnp
from jax import lax
from jax.experimental import pallas as pl
from jax.experimental.pallas import tpu as pltpu

F32, BF16, I32 = jnp.float32, jnp.bfloat16, jnp.int32

CHUNK = 64
N_META = 16
HEAD_DIM = 64
HEAD_W = 2 * HEAD_DIM
POOL_WINDOWS = (2, 4, 8, 16)
POOL_HALO = 16
N_EXPERTS = 16
N_GROUPS = 4
EPG = N_EXPERTS // N_GROUPS
PAIRS = ((0, 1), (0, 2), (0, 3), (1, 2), (1, 3), (2, 3))
N_CLASSES = N_GROUPS * len(PAIRS)
LN_EPS = 1e-5
NEG = -1e30
LOG2E = 1.4426950408889634
Q_SCALE = HEAD_DIM ** -0.5 * LOG2E

TM = 512
TMOE = 256
MOE_CHUNKS = 4
ATT_T = 256
ATT_PAD = 128
VMEM_LIMIT = 56 * 1024 * 1024


def _cparams(*sem):
    return pltpu.CompilerParams(dimension_semantics=sem, vmem_limit_bytes=VMEM_LIMIT)


def _ln(x, g, b):
    mu = jnp.mean(x, axis=-1, keepdims=True)
    xc = x - mu
    var = jnp.mean(xc * xc, axis=-1, keepdims=True)
    return xc * lax.rsqrt(var + LN_EPS) * g + b


def _dot(a, b):
    return jnp.dot(a, b, preferred_element_type=F32)


def _dot_nt(a, b):
    return lax.dot_general(a, b, (((1,), (1,)), ((), ())), preferred_element_type=F32)


def _k1_project(norm, w_ref, h_ref, q_ref, u_ref, kb_ref, vb_ref, k_ref, v_ref, kx_ref, vx_ref, is_extras):
    aw = kx_ref.shape[-1]
    half = TM // 2
    for r0 in (0, half):
        rows = slice(r0, r0 + half)
        hb = norm(rows).astype(BF16)
        h_ref[rows, :] = hb
        q_ref[rows, :] = (_dot(hb, w_ref[:, 0:aw]) * Q_SCALE).astype(BF16)
        k = _dot(hb, w_ref[:, aw:2 * aw])
        v = _dot(hb, w_ref[:, 2 * aw:3 * aw])
        u_ref[rows, :] = _dot(hb, w_ref[:, 3 * aw:])
        kb_ref[rows, :] = k.astype(BF16)
        vb_ref[rows, :] = v.astype(BF16)
        k_ref[rows] = k.reshape((half,) + k_ref.shape[1:])
        v_ref[rows] = v.reshape((half,) + v_ref.shape[1:])
        if is_extras:
            kx_ref[rows, :] = k
            vx_ref[rows, :] = v


def _k1_first_body(xe_ref, x_ref, g_ref, b_ref, w_ref, h_ref, q_ref, u_ref, kb_ref, vb_ref, k_ref, v_ref, kx_ref,
                   vx_ref):
    outs = (w_ref, h_ref, q_ref, u_ref, kb_ref, vb_ref, k_ref, v_ref, kx_ref, vx_ref)

    @pl.when(pl.program_id(0) == 0)
    def _():
        _k1_project(lambda rows: _ln(xe_ref[rows, :], g_ref[...], b_ref[...]), *outs, True)

    @pl.when(pl.program_id(0) > 0)
    def _():
        _k1_project(lambda rows: _ln(x_ref[rows, :], g_ref[...], b_ref[...]), *outs, False)


def _k1_next_body(alpha, h1_ref, y_ref, g_ref, b_ref, w_ref, ko_ref, vo_ref,
                  h_ref, q_ref, u_ref, kb_ref, vb_ref, k_ref, v_ref, kx_ref, vx_ref):
    del ko_ref, vo_ref
    outs = (w_ref, h_ref, q_ref, u_ref, kb_ref, vb_ref, k_ref, v_ref, kx_ref, vx_ref)

    def norm(rows):
        return _ln(alpha * h1_ref[rows, :] + y_ref[rows, :].astype(F32), g_ref[...], b_ref[...])

    @pl.when(pl.program_id(0) == 0)
    def _():
        _k1_project(norm, *outs, True)

    @pl.when(pl.program_id(0) > 0)
    def _():
        _k1_project(norm, *outs, False)


def _k1_specs(layer, n_tiles, tiles_per_b, d, aw, pw):
    def frames_tile(i):
        return jnp.maximum(i - 1, 0)

    def kv_map(i):
        t = frames_tile(i)
        return (layer, t // tiles_per_b, pl.multiple_of(N_META + (t % tiles_per_b) * TM, 16), 0, 0)

    kv_spec = pl.BlockSpec((None, None, pl.Element(TM), pl.Element(aw // HEAD_W), pl.Element(HEAD_W)), kv_map)
    row = lambda w: pl.BlockSpec((TM, w), lambda i: (i, 0))
    const = lambda r, w: pl.BlockSpec((r, w), lambda i: (0, 0))
    out_specs = [row(d), row(aw), row(pw), row(aw), row(aw), kv_spec, kv_spec, const(TM, aw), const(TM, aw)]
    return row, const, frames_tile, out_specs


def _k1_out_shapes(n, d, aw, pw, depth, b, l):
    return [
        jax.ShapeDtypeStruct((n, d), BF16),
        jax.ShapeDtypeStruct((n, aw), BF16),
        jax.ShapeDtypeStruct((n, pw), F32),
        jax.ShapeDtypeStruct((n, aw), BF16),
        jax.ShapeDtypeStruct((n, aw), BF16),
        jax.ShapeDtypeStruct((depth, b, l, aw // HEAD_W, HEAD_W), F32),
        jax.ShapeDtypeStruct((depth, b, l, aw // HEAD_W, HEAD_W), F32),
        jax.ShapeDtypeStruct((TM, aw), F32),
        jax.ShapeDtypeStruct((TM, aw), F32),
    ]


def _k1_first(x_extra, x_frames, g, b, w, depth, bsz, seq):
    nf, d = x_frames.shape
    aw = w.shape[1] // 4
    pw = w.shape[1] - 3 * aw
    n = TM + nf
    n_tiles = n // TM
    tpb = seq // TM
    row, const, frames_tile, out_specs = _k1_specs(0, n_tiles, tpb, d, aw, pw)
    in_specs = [const(TM, d), pl.BlockSpec((TM, d), lambda i: (frames_tile(i), 0)),
                const(1, d), const(1, d), const(d, w.shape[1])]
    return pl.pallas_call(
        _k1_first_body, grid=(n_tiles,), in_specs=in_specs, out_specs=out_specs,
        out_shape=_k1_out_shapes(n, d, aw, pw, depth, bsz, N_META + seq),
        compiler_params=_cparams("arbitrary"), name="k1_first",
    )(x_extra, x_frames, g, b, w)


def _k1_next(layer, alpha, h1, y, g, b, w, k_out, v_out, seq):
    n, d = h1.shape
    depth, bsz, l, n_heads, _ = k_out.shape
    aw = n_heads * HEAD_W
    pw = w.shape[1] - 3 * aw
    n_tiles = n // TM
    tpb = seq // TM
    row, const, _, out_specs = _k1_specs(layer, n_tiles, tpb, d, aw, pw)
    any_spec = pl.BlockSpec(memory_space=pl.ANY)
    in_specs = [row(d), row(d), const(1, d), const(1, d), const(d, w.shape[1]), any_spec, any_spec]
    return pl.pallas_call(
        functools.partial(_k1_next_body, alpha), grid=(n_tiles,), in_specs=in_specs, out_specs=out_specs,
        out_shape=_k1_out_shapes(n, d, aw, pw, depth, bsz, l),
        input_output_aliases={5: 5, 6: 6},
        compiler_params=_cparams("arbitrary"), name="k1_next",
    )(h1, y, g, b, w, k_out, v_out)


def _bf16_part(x):
    bits = lax.bitcast_convert_type(x, jnp.uint32) & jnp.uint32(0xFFFF0000)
    return lax.bitcast_convert_type(bits, F32)


def _split3(c):
    hi = _bf16_part(c)
    r = c - hi
    mid = _bf16_part(r)
    return hi, mid, r - mid


def _alibi_tables(slopes, seq):
    pad = ATT_PAD - N_META
    pos = jnp.concatenate([jnp.zeros((pad,), F32), jnp.arange(N_META + seq, dtype=F32)])
    live = jnp.arange(ATT_PAD + seq) >= pad
    c = jnp.where(live[None, :], (slopes * LOG2E)[:, None] * pos[None, :], NEG)
    bias = jnp.pad(jnp.stack(_split3(c), axis=-1), ((0, 0), (0, 0), (0, HEAD_W - 3))).astype(BF16)
    rr = lax.broadcasted_iota(I32, (ATT_T, ATT_T), 0)
    cc = lax.broadcasted_iota(I32, (ATT_T, ATT_T), 1)
    ahead = jnp.maximum(cc - rr, 0).astype(F32)
    diag = jnp.where(((cc // CHUNK) <= (rr // CHUNK))[None], (-2.0 * LOG2E * slopes)[:, None, None] * ahead[None], NEG)
    return bias, diag


def _attn_body(lam_ref, q_ref, kb_ref, vb_ref, kx_ref, vx_ref, bias_ref, diag_ref, sw_ref, o_ref, ka, vm, qa1, qa2,
               sc):
    t = ATT_T
    lam = lam_ref[0]
    s_len = q_ref.shape[0]
    n_sub = s_len // t
    kp = ATT_PAD
    pad = kp - N_META
    lane = lax.broadcasted_iota(I32, (1, HEAD_W), 1)
    first_half = (lane < HEAD_DIM).astype(BF16)
    ones = jnp.broadcast_to((lane < 3).astype(BF16), (s_len, HEAD_W))

    ka[0:pad, 0:HEAD_W] = jnp.zeros((pad, HEAD_W), BF16)
    ka[pad:kp, 0:HEAD_W] = kx_ref[...].astype(BF16)
    ka[kp:, 0:HEAD_W] = kb_ref[...]
    ka[:, HEAD_W:] = bias_ref[...]
    vm[0:pad, :] = jnp.zeros((pad, HEAD_W), BF16)
    vm[pad:kp, :] = vx_ref[...].astype(BF16)
    qb = q_ref[...]
    qa1[:, 0:HEAD_W] = qb * first_half
    qa2[:, 0:HEAD_W] = qb * (1 - first_half)
    qa1[:, HEAD_W:] = ones
    qa2[:, HEAD_W:] = ones

    def tiles(sb):
        return [(0, kp)] + [(kp + j * t, t) for j in range(sb + 1)]

    def fold(x, op):
        return x if x.shape[1] == HEAD_W else op(x[:, :HEAD_W], x[:, HEAD_W:])

    def scores(sb):
        row_max = []
        for mp, qa in enumerate((qa1, qa2)):
            q = qa[sb * t:(sb + 1) * t, :]
            macc = None
            tl = tiles(sb)
            for i, (c0, w) in enumerate(tl):
                s = _dot_nt(q, ka[c0:c0 + w, :])
                if i == len(tl) - 1:
                    s = s + diag_ref[...]
                sc[sb % 2, mp, :, c0:c0 + w] = s
                f = fold(s, jnp.maximum)
                macc = f if macc is None else jnp.maximum(macc, f)
            row_max.append(jnp.max(macc, axis=-1, keepdims=True))
        return row_max

    nxt = scores(0)
    for sb in range(n_sub):
        q0 = sb * t
        row_max = nxt
        if sb + 1 < n_sub:
            nxt = scores(sb + 1)
        ls = []
        for mp in range(2):
            mb = jnp.broadcast_to(row_max[mp], (t, HEAD_W))
            lacc = None
            for c0, w in tiles(sb):
                s = sc[sb % 2, mp, :, c0:c0 + w]
                p = jnp.exp2(s - (mb if w == HEAD_W else jnp.concatenate([mb, mb], axis=-1)))
                sc[sb % 2, mp, :, c0:c0 + w] = p
                f = fold(p, jnp.add)
                lacc = f if lacc is None else lacc + f
            ls.append(jnp.sum(lacc, axis=-1, keepdims=True))
        fac = lam * ls[0] / ls[1]
        acc = None
        for c0, w in tiles(sb):
            wt = (sc[sb % 2, 0, :, c0:c0 + w] - sc[sb % 2, 1, :, c0:c0 + w] * fac).astype(BF16)
            d = _dot(wt, vm[...] if c0 == 0 else vb_ref[c0 - kp:c0 - kp + w, :])
            acc = d if acc is None else acc + d
        o = acc / ls[0]
        o = o * lax.rsqrt(jnp.mean(o * o, axis=-1, keepdims=True) + LN_EPS)
        o_ref[q0:q0 + t, :] = (o * sw_ref[...]).astype(BF16)


def _prompt_attention(lam, q, kb, vb, kx, vx, bias, diag, sw, bsz, seq, meta_row):
    n, aw = q.shape
    n_heads = aw // HEAD_W
    key_rows = ATT_PAD + seq
    smem = pl.BlockSpec(memory_space=pltpu.SMEM)
    frames = pl.BlockSpec((pl.Element(seq), pl.Element(HEAD_W)),
                          lambda h, b: (pl.multiple_of(TM + b * seq, TM), pl.multiple_of(h * HEAD_W, HEAD_W)))
    meta = pl.BlockSpec((N_META, HEAD_W), lambda h, b: (meta_row // N_META, h))
    return pl.pallas_call(
        _attn_body, grid=(n_heads, bsz),
        in_specs=[smem, frames, frames, frames, meta, meta,
                  pl.BlockSpec((None, key_rows, HEAD_W), lambda h, b: (h, 0, 0)),
                  pl.BlockSpec((None, ATT_T, ATT_T), lambda h, b: (h, 0, 0)),
                  pl.BlockSpec((1, HEAD_W), lambda h, b: (0, 0))],
        out_specs=frames,
        out_shape=jax.ShapeDtypeStruct((n, aw), BF16),
        scratch_shapes=[pltpu.VMEM((key_rows, 2 * HEAD_W), BF16), pltpu.VMEM((ATT_PAD, HEAD_W), BF16),
                        pltpu.VMEM((seq, 2 * HEAD_W), BF16), pltpu.VMEM((seq, 2 * HEAD_W), BF16),
                        pltpu.VMEM((2, 2, ATT_T, key_rows), F32)],
        compiler_params=_cparams("arbitrary", "arbitrary"), name="prompt_attention",
    )(lam, q, kb, vb, kx, vx, bias, diag, sw)


def _pool_mix(ext_ref, n_rows, counts, wp_ref, ps_ref, row0=0):
    gw = wp_ref.shape[-1]
    outs = []
    base = POOL_HALO + row0
    for g, w in enumerate(POOL_WINDOWS):
        cols = slice(g * gw, (g + 1) * gw)
        tok = ext_ref[base:base + n_rows, cols]
        acc = tok
        for i in range(1, w):
            acc = acc + ext_ref[base - i:base - i + n_rows, cols]
        if counts is None:
            d = acc * (1.0 / w) - tok
        else:
            d = acc / jnp.minimum(counts, float(w)) - tok
        outs.append(_dot(d.astype(BF16), wp_ref[g]))
    return jnp.concatenate(outs, axis=-1) * ps_ref[...]


def _route(h1, r1_ref, rb_ref, cls_ref, ga_ref, gb_ref):
    hi = h1.astype(BF16)
    lo = (h1 - hi.astype(F32)).astype(BF16)
    at = _dot(hi, r1_ref[...]).T
    bt = _dot(lo, r1_ref[...]).T
    logit = at[0:N_EXPERTS] + at[N_EXPERTS:2 * N_EXPERTS] + bt[0:N_EXPERTS]
    lg = [logit[e:e + 1, :] for e in range(N_EXPERTS)]
    m = functools.reduce(jnp.maximum, lg)
    ex = [jnp.exp(x - m) for x in lg]
    z = functools.reduce(jnp.add, ex)
    prob = [x / z for x in ex]
    sel = [prob[e] + rb_ref[e] for e in range(N_EXPERTS)]

    def top2_sum(v):
        best = None
        for a, b in PAIRS:
            s = v[a] + v[b]
            best = s if best is None else jnp.maximum(best, s)
        return best

    score = [top2_sum(sel[g * EPG:(g + 1) * EPG]) for g in range(N_GROUPS)]
    gi = jnp.zeros_like(score[0], dtype=I32)
    best = score[0]
    for g in range(1, N_GROUPS):
        better = score[g] > best
        gi = jnp.where(better, g, gi)
        best = jnp.where(better, score[g], best)

    def pick(vals, j):
        out = vals[j]
        for g in range(1, N_GROUPS):
            out = jnp.where(gi == g, vals[g * EPG + j], out)
        return out

    sg = [pick(sel, j) for j in range(EPG)]
    pg = [pick(prob, j) for j in range(EPG)]

    def argmax_first(v, skip=None):
        bi = jnp.zeros_like(gi)
        bv = None
        for j in range(EPG):
            x = v[j] if skip is None else jnp.where(skip == j, -jnp.inf, v[j])
            if bv is None:
                bv = x
            else:
                better = x > bv
                bi = jnp.where(better, j, bi)
                bv = jnp.where(better, x, bv)
        return bi

    e1 = argmax_first(sg)
    e2 = argmax_first(sg, skip=e1)

    def at_idx(v, idx):
        out = v[0]
        for j in range(1, EPG):
            out = jnp.where(idx == j, v[j], out)
        return out

    g1 = at_idx(pg, e1)
    g2 = at_idx(pg, e2)
    tot = g1 + g2
    g1 = g1 / tot
    g2 = g2 / tot
    swap = e2 < e1
    ea = jnp.where(swap, e2, e1)
    eb = jnp.where(swap, e1, e2)
    pair = jnp.zeros_like(gi)
    for p, (a, b) in enumerate(PAIRS):
        pair = jnp.where((ea == a) & (eb == b), p, pair)
    cls_ref[...] = gi * len(PAIRS) + pair
    ga_ref[...] = jnp.where(swap, g2, g1)
    gb_ref[...] = jnp.where(swap, g1, g2)


def _post(alpha, mix, h, wo_ref, g_ref, b_ref, r1_ref, rb_ref, h1_ref, cls_ref, ga_ref, gb_ref):
    h1 = _ln(alpha * h.astype(F32) + _dot(mix, wo_ref[...]), g_ref[...], b_ref[...])
    h1_ref[...] = h1
    _route(h1, r1_ref, rb_ref, cls_ref, ga_ref, gb_ref)


def _post_out_shapes(n, d):
    nt = n // TM
    return [jax.ShapeDtypeStruct((n, d), F32), jax.ShapeDtypeStruct((nt, 1, TM), I32),
            jax.ShapeDtypeStruct((nt, 1, TM), F32), jax.ShapeDtypeStruct((nt, 1, TM), F32)]


def _kx_body(alpha, n_dec, t_dec, slope_ref, lam_ref, rb_ref, q_ref, kx_ref, vx_ref, ux_ref, ck_ref, cv_ref,
             sp_ref, sw_ref, wp_ref, ps_ref, h_ref, wo_ref, g_ref, b_ref, r1_ref,
             h1_ref, cls_ref, ga_ref, gb_ref, npool_ref, mix, ext):
    s = pl.program_id(0)
    past = ck_ref.shape[0]
    aw = q_ref.shape[1]
    n_heads = aw // HEAD_W
    is_meta = s == n_dec
    lam = lam_ref[0]
    r0 = pl.multiple_of(s * t_dec, t_dec)

    @pl.when(s == 0)
    def _():
        mix[...] = jnp.zeros_like(mix)

    q = q_ref[pl.ds(r0, t_dec), :].astype(F32)
    kn = kx_ref[pl.ds(r0, t_dec), :]
    vn = vx_ref[pl.ds(r0, t_dec), :]
    rq = lax.broadcasted_iota(I32, (t_dec, past), 0)
    cp = lax.broadcasted_iota(I32, (t_dec, past), 1)
    dist_p = (past + rq - cp).astype(F32)
    rn = lax.broadcasted_iota(I32, (t_dec, t_dec), 0)
    cn = lax.broadcasted_iota(I32, (t_dec, t_dec), 1)
    dist_n = jnp.abs(rn - cn).astype(F32)
    heads = []
    for hd in range(n_heads):
        slope = slope_ref[hd] * LOG2E
        c0 = hd * HEAD_W
        ckh = ck_ref[:, hd, :].astype(BF16)
        cvh = cv_ref[:, hd, :].astype(BF16)
        maps = []
        for half in range(2):
            lo_c = c0 + half * HEAD_DIM
            qh = q[:, lo_c:lo_c + HEAD_DIM].astype(BF16)
            sp = _dot_nt(qh, ckh[:, half * HEAD_DIM:(half + 1) * HEAD_DIM]) - slope * dist_p
            sp = jnp.where(is_meta, NEG, sp)
            sn = _dot_nt(qh, kn[:, lo_c:lo_c + HEAD_DIM].astype(BF16)) - slope * dist_n
            m = jnp.maximum(jnp.max(sp, axis=-1, keepdims=True), jnp.max(sn, axis=-1, keepdims=True))
            pp = jnp.exp2(sp - m)
            pn = jnp.exp2(sn - m)
            z = jnp.sum(pp, axis=-1, keepdims=True) + jnp.sum(pn, axis=-1, keepdims=True)
            maps.append((pp / z, pn / z))
        wp_ = (maps[0][0] - lam * maps[1][0]).astype(BF16)
        wn_ = (maps[0][1] - lam * maps[1][1]).astype(BF16)
        o = _dot(wp_, cvh) + _dot(wn_, vn[:, c0:c0 + HEAD_W].astype(BF16))
        o = o * lax.rsqrt(jnp.mean(o * o, axis=-1, keepdims=True) + LN_EPS)
        heads.append(o * sw_ref[...])
    attn = jnp.concatenate(heads, axis=-1)

    hist = jnp.where(is_meta, 0.0, sp_ref[...])
    u = ux_ref[pl.ds(r0, t_dec), :]
    ext[POOL_HALO - sp_ref.shape[0]:POOL_HALO, :] = hist
    ext[POOL_HALO:POOL_HALO + t_dec, :] = u
    pos = lax.broadcasted_iota(I32, (t_dec, 1), 0) + jnp.where(is_meta, 0, past)
    pool = _pool_mix(ext, t_dec, (pos + 1).astype(F32), wp_ref, ps_ref)
    mix[pl.ds(r0, t_dec), :] = jnp.concatenate([attn, pool], axis=-1).astype(BF16)
    keep = sp_ref.shape[0]
    npool_ref[...] = ext[POOL_HALO + t_dec - keep:POOL_HALO + t_dec, :]

    @pl.when(s == n_dec)
    def _():
        _post(alpha, mix[...], h_ref[...], wo_ref, g_ref, b_ref, r1_ref, rb_ref,
              h1_ref, cls_ref, ga_ref, gb_ref)


def _extras(layer, alpha, slopes, lam, rb, q, kx, vx, u, cache_k, cache_v, state_pool, sw, wp, ps, h, wo, g, b,
            r1, n_dec, t_dec):
    n, d = h.shape
    aw = q.shape[1]
    pw = u.shape[1]
    past = cache_k.shape[2]
    keep = state_pool.shape[2]
    smem = pl.BlockSpec(memory_space=pltpu.SMEM)
    tile0 = lambda w: pl.BlockSpec((TM, w), lambda s: (0, 0))
    full = lambda *shape: pl.BlockSpec(shape, lambda s: (0,) * len(shape))
    dec = lambda s: jnp.minimum(s, n_dec - 1)
    cache_spec = pl.BlockSpec((None, None, past, aw // HEAD_W, HEAD_W), lambda s: (layer, dec(s), 0, 0, 0))
    small = pl.BlockSpec((None, 1, TM), lambda s: (0, 0, 0))
    out_shapes = _post_out_shapes(n, d) + [jax.ShapeDtypeStruct((n_dec + 1, keep, pw), F32)]
    out_specs = [tile0(d), small, small, small, pl.BlockSpec((None, keep, pw), lambda s: (s, 0, 0))]
    return pl.pallas_call(
        functools.partial(_kx_body, alpha, n_dec, t_dec), grid=(n_dec + 1,),
        in_specs=[smem, smem, smem, tile0(aw), full(TM, aw), full(TM, aw), tile0(pw), cache_spec, cache_spec,
                  pl.BlockSpec((None, None, keep, pw), lambda s: (layer, dec(s), 0, 0)),
                  full(1, HEAD_W), full(*wp.shape), full(1, pw), tile0(d), full(*wo.shape), full(1, d), full(1, d),
                  full(*r1.shape)],
        out_specs=out_specs, out_shape=out_shapes,
        scratch_shapes=[pltpu.VMEM((TM, aw + pw), BF16), pltpu.VMEM((POOL_HALO + t_dec, pw), F32)],
        compiler_params=_cparams("arbitrary"), name="extras_mix",
    )(slopes, lam, rb, q, kx, vx, u, cache_k, cache_v, state_pool, sw, wp, ps, h, wo, g, b, r1)


def _k3_body(alpha, tiles_per_b, rb_ref, a_ref, u_ref, uprev_ref, umeta_ref, wp_ref, ps_ref, h_ref, wo_ref, g_ref,
             b_ref, r1_ref, h1_in, cls_in, ga_in, gb_in, h1_ref, cls_ref, ga_ref, gb_ref, ext):
    del h1_in, cls_in, ga_in, gb_in
    first = (pl.program_id(0) % tiles_per_b) == 0
    ext[0:POOL_HALO, :] = jnp.where(first, umeta_ref[...], uprev_ref[...])
    ext[POOL_HALO:, :] = u_ref[...]
    half = TM // 2
    pre = []
    for r0 in (0, half):
        rows = slice(r0, r0 + half)
        pool = _pool_mix(ext, half, None, wp_ref, ps_ref, row0=r0)
        mix = jnp.concatenate([a_ref[rows, :], pool.astype(BF16)], axis=-1)
        pre.append(alpha * h_ref[rows, :].astype(F32) + _dot(mix, wo_ref[...]))
    for r0, z in zip((0, half), pre):
        rows = slice(r0, r0 + half)
        h1 = _ln(z, g_ref[...], b_ref[...])
        h1_ref[rows, :] = h1
        _route(h1, r1_ref, rb_ref, cls_ref.at[:, rows], ga_ref.at[:, rows], gb_ref.at[:, rows])


def _frames_mix(alpha, rb, a, u, wp, ps, h, wo, g, b, r1, prev_outs, seq, meta_row):
    n, d = h.shape
    aw, pw = a.shape[1], u.shape[1]
    n_tiles = n // TM - 1
    tpb = seq // TM
    smem = pl.BlockSpec(memory_space=pltpu.SMEM)
    row = lambda w: pl.BlockSpec((TM, w), lambda i: (i + 1, 0))
    full = lambda *shape: pl.BlockSpec(shape, lambda i: (0,) * len(shape))
    hb = TM // POOL_HALO
    small = pl.BlockSpec((None, 1, TM), lambda i: (i + 1, 0, 0))
    any_spec = pl.BlockSpec(memory_space=pl.ANY)
    return pl.pallas_call(
        functools.partial(_k3_body, alpha, tpb), grid=(n_tiles,),
        in_specs=[smem, row(aw), row(pw),
                  pl.BlockSpec((POOL_HALO, pw), lambda i: ((i + 1) * hb - 1, 0)),
                  pl.BlockSpec((POOL_HALO, pw), lambda i: (meta_row // POOL_HALO, 0)),
                  full(*wp.shape), full(1, pw), row(d), full(*wo.shape), full(1, d), full(1, d), full(*r1.shape)]
                 + [any_spec] * 4,
        out_specs=[row(d), small, small, small],
        out_shape=_post_out_shapes(n, d),
        input_output_aliases={12: 0, 13: 1, 14: 2, 15: 3},
        scratch_shapes=[pltpu.VMEM((POOL_HALO + TM, pw), F32)],
        compiler_params=_cparams("arbitrary"), name="frames_mix",
    )(rb, a, u, u, u, wp, ps, h, wo, g, b, r1, *prev_outs)


def _moe_body(ea_ref, eb_ref, valid_ref, x_ref, ga_ref, gb_ref, wga_f, wua_f, wda_f, wgb_f, wub_f, wdb_f, ys_in,
              y_ref, wga, wua, wda, wgb, wub, wdb):
    del ys_in
    t = pl.program_id(0)
    prev = jnp.maximum(t - 1, 0)
    fresh = (t == 0) | (ea_ref[t] != ea_ref[prev]) | (eb_ref[t] != eb_ref[prev])

    @pl.when(fresh)
    def _():
        for src, dst in ((wga_f, wga), (wua_f, wua), (wda_f, wda), (wgb_f, wgb), (wub_f, wub), (wdb_f, wdb)):
            dst[...] = src[...].astype(BF16)

    @pl.when(valid_ref[pl.program_id(0)] > 0)
    def _():
        x = x_ref[...].astype(BF16)
        rows = x.shape[0]

        def lane_bcast(g_ref):
            return jnp.broadcast_to(g_ref[...], (HEAD_W, rows)).T

        def expert(wg, wu, wd, gate):
            gt = _dot(x, wg[...])
            hid = gt * (1.0 / (1.0 + jnp.exp(-gt))) * _dot(x, wu[...])
            reps = hid.shape[1] // HEAD_W
            hid = hid * jnp.concatenate([gate] * reps, axis=-1)
            return _dot(hid.astype(BF16), wd[...])

        y = expert(wga, wua, wda, lane_bcast(ga_ref))
        y_ref[...] = (y + expert(wgb, wub, wdb, lane_bcast(gb_ref))).astype(y_ref.dtype)

    @pl.when(valid_ref[pl.program_id(0)] == 0)
    def _():
        y_ref[...] = jnp.zeros_like(y_ref)


def _grouped_experts(layer, tile0, n_pad, ea, eb, valid, xs, gas, gbs, w_gate, w_up, w_down, ys):
    rows_here, d = xs.shape
    n_tiles = rows_here // TMOE
    de = w_gate.shape[3]
    wa = lambda r, c: pl.BlockSpec((None, None, r, c), lambda t, ea, eb, valid: (layer, ea[t], 0, 0))
    wb = lambda r, c: pl.BlockSpec((None, None, r, c), lambda t, ea, eb, valid: (layer, eb[t], 0, 0))
    gate_spec = pl.BlockSpec((None, 1, TMOE), lambda t, ea, eb, valid: (t, 0, 0))
    in_rows = pl.BlockSpec((TMOE, d), lambda t, ea, eb, valid: (t, 0))
    out_rows = pl.BlockSpec((TMOE, d), lambda t, ea, eb, valid: (t + tile0, 0))
    if ys is None:
        ys = jnp.zeros((8, HEAD_W), BF16)
        aliases = {}
    else:
        aliases = {12: 0}
    grid_spec = pltpu.PrefetchScalarGridSpec(
        num_scalar_prefetch=3, grid=(n_tiles,),
        in_specs=[in_rows, gate_spec, gate_spec, wa(d, de), wa(d, de), wa(de, d), wb(d, de), wb(d, de), wb(de, d),
                  pl.BlockSpec(memory_space=pl.ANY)],
        out_specs=out_rows,
        scratch_shapes=[pltpu.VMEM((d, de), BF16), pltpu.VMEM((d, de), BF16), pltpu.VMEM((de, d), BF16)] * 2,
    )
    return pl.pallas_call(
        _moe_body, grid_spec=grid_spec, out_shape=jax.ShapeDtypeStruct((n_pad, d), BF16),
        input_output_aliases=aliases,
        compiler_params=_cparams("arbitrary"), name="grouped_experts",
    )(ea, eb, valid, xs, gas.reshape(n_tiles, 1, TMOE), gbs.reshape(n_tiles, 1, TMOE),
      w_gate, w_up, w_down, w_gate, w_up, w_down, ys)


def _rows(x, idx):
    return x.at[idx].get(mode="promise_in_bounds")


def _dispatch_plan(cls):
    n = cls.shape[0]
    n_pad = n + N_CLASSES * TMOE
    classes = jnp.arange(N_CLASSES, dtype=I32)
    onehot = (cls[:, None] == classes[None, :]).astype(I32)
    rank = jnp.sum(jnp.cumsum(onehot, axis=0) * onehot, axis=1) - 1
    counts = jnp.sum(onehot, axis=0)
    padded = (counts + TMOE - 1) // TMOE * TMOE
    ends = jnp.cumsum(padded)
    starts = ends - padded
    first = jnp.cumsum(counts) - counts
    pos = jnp.sum(onehot * starts[None, :], axis=1) + rank
    shift = max(n - 1, 1).bit_length()
    order = jnp.sort(cls * (1 << shift) + jnp.arange(n, dtype=I32)) & ((1 << shift) - 1)
    row = jnp.arange(n_pad, dtype=I32)
    row_cls = jnp.minimum(jnp.sum((ends[None, :] <= row[:, None]).astype(I32), axis=1), N_CLASSES - 1)
    row_hot = (row_cls[:, None] == classes[None, :]).astype(I32)
    within = row - jnp.sum(row_hot * starts[None, :], axis=1)
    live = within < jnp.sum(row_hot * counts[None, :], axis=1)
    src = _rows(order, jnp.clip(jnp.sum(row_hot * first[None, :], axis=1) + within, 0, n - 1))
    src = jnp.where(live, src, row % n)
    tile_cls = row_cls[::TMOE]
    valid = (row[::TMOE] < ends[-1]).astype(I32)
    pa = jnp.array([p[0] for p in PAIRS], I32)
    pb = jnp.array([p[1] for p in PAIRS], I32)
    grp = tile_cls // len(PAIRS)
    pair_hot = ((tile_cls % len(PAIRS))[:, None] == jnp.arange(len(PAIRS), dtype=I32)[None, :]).astype(I32)
    ea = grp * EPG + jnp.sum(pair_hot * pa[None, :], axis=1)
    eb = grp * EPG + jnp.sum(pair_hot * pb[None, :], axis=1)
    return pos.astype(I32), src, live, ea.astype(I32), eb.astype(I32), valid


def _moe(layer, h1, cls, ga, gb, w_gate, w_up, w_down):
    pos, src, live, ea, eb, valid = _dispatch_plan(cls)
    gas = jnp.where(live, _rows(ga, src), 0.0)
    gbs = jnp.where(live, _rows(gb, src), 0.0)
    n_pad = src.shape[0]
    n_tiles = n_pad // TMOE
    cuts = [n_tiles * i // MOE_CHUNKS for i in range(MOE_CHUNKS + 1)]
    ys = None
    for t0, t1 in zip(cuts[:-1], cuts[1:]):
        r = slice(t0 * TMOE, t1 * TMOE)
        ys = _grouped_experts(layer, t0, n_pad, ea[t0:t1], eb[t0:t1], valid[t0:t1], _rows(h1, src[r]), gas[r], gbs[r],
                              w_gate, w_up, w_down, ys)
    return _rows(ys, pos)


def _tail_body(alpha, h1_ref, y_ref, g_ref, b_ref, of_ref, ox_ref):
    out = _ln(alpha * h1_ref[...] + y_ref[...].astype(F32), g_ref[...], b_ref[...])
    of_ref[...] = out

    @pl.when(pl.program_id(0) == 0)
    def _():
        ox_ref[...] = out


def _tail(alpha, h1, y, g, b):
    n, d = h1.shape
    row = pl.BlockSpec((TM, d), lambda i: (i, 0))
    const = pl.BlockSpec((1, d), lambda i: (0, 0))
    return pl.pallas_call(
        functools.partial(_tail_body, alpha), grid=(n // TM,),
        in_specs=[row, row, const, const],
        out_specs=[pl.BlockSpec((TM, d), lambda i: (jnp.maximum(i - 1, 0), 0)),
                   pl.BlockSpec((TM, d), lambda i: (0, 0))],
        out_shape=[jax.ShapeDtypeStruct((n - TM, d), F32), jax.ShapeDtypeStruct((TM, d), F32)],
        compiler_params=_cparams("arbitrary"), name="final_norm",
    )(h1, y, g, b)


def kernel(x_prompt, x_sample, cache_k, cache_v, state_pool, meta_tokens, ln_in_g, ln_in_b, w_in, lambda_qk,
           subln_w, w_pool, pool_scale, w_out, ln1_g, ln1_b, router_w, router_b, w_gate, w_up, w_down,
           ln2_g, ln2_b):
    bsz, seq, d = x_prompt.shape
    n_dec, t_dec, _ = x_sample.shape
    depth = w_in.shape[0]
    past = cache_k.shape[2]
    n_heads = cache_k.shape[3]
    aw = n_heads * HEAD_W
    alpha = (2 * depth) ** 0.25
    n_sample = n_dec * t_dec
    assert seq % TM == 0 and n_sample + N_META <= TM and t_dec == N_META and n_sample % POOL_HALO == 0
    assert past % CHUNK == 0 and t_dec <= CHUNK

    x_extra = jnp.concatenate([x_sample.reshape(n_sample, d), meta_tokens.astype(F32),
                               jnp.zeros((TM - n_sample - N_META, d), F32)], axis=0)
    x_frames = x_prompt.reshape(bsz * seq, d)
    row = lambda v: v.reshape(1, -1).astype(F32)
    slopes = (2.0 ** (-8.0 * jnp.arange(1, n_heads + 1, dtype=F32) / n_heads)).astype(F32)
    bias, diag = _alibi_tables(slopes, seq)
    rw_hi = _bf16_part(router_w.astype(F32))
    rw_lo = router_w.astype(F32) - rw_hi
    r1 = jnp.concatenate([rw_hi, rw_lo, jnp.zeros((d, HEAD_W - 2 * N_EXPERTS), F32)], axis=1).astype(BF16)
    rb = router_b.astype(F32)

    k_out = v_out = None
    h1 = y = None
    pools_p, pools_s, ks_s, vs_s = [], [], [], []
    for l in range(depth):
        lam_init = 0.8 - 0.6 * math.exp(-0.3 * l)
        lq = lambda_qk[l].astype(F32)
        lam = (jnp.exp(jnp.sum(lq[0] * lq[1])) - jnp.exp(jnp.sum(lq[2] * lq[3])) + lam_init).reshape(1)
        sw = row(subln_w[l]) * (1.0 - lam_init)
        w_in_l = w_in[l].astype(BF16)
        if l == 0:
            k1_outs = _k1_first(x_extra, x_frames, row(ln_in_g), row(ln_in_b), w_in_l, depth, bsz, seq)
        else:
            k1_outs = _k1_next(l, alpha, h1, y, row(ln2_g[l - 1]), row(ln2_b[l - 1]), w_in_l, k_out, v_out, seq)
        h, q, u, kb, vb, k_out, v_out, kx, vx = k1_outs
        meta_shape = (bsz, N_META, n_heads, HEAD_W)
        meta_k = jnp.broadcast_to(kx[n_sample:n_sample + N_META].reshape(1, *meta_shape[1:]), meta_shape)
        meta_v = jnp.broadcast_to(vx[n_sample:n_sample + N_META].reshape(1, *meta_shape[1:]), meta_shape)
        k_out = k_out.at[l, :, :N_META].set(meta_k)
        v_out = v_out.at[l, :, :N_META].set(meta_v)
        a = _prompt_attention(lam, q, kb, vb, kx, vx, bias, diag, sw, bsz, seq, n_sample)
        wp = w_pool[l].astype(BF16)
        ps = row(pool_scale[l])
        wo = w_out[l].astype(BF16)
        g1, b1 = row(ln1_g[l]), row(ln1_b[l])
        *x_outs, npool = _extras(l, alpha, slopes, lam, rb, q, kx, vx, u, cache_k, cache_v, state_pool, sw, wp, ps, h, wo,
                                 g1, b1, r1, n_dec, t_dec)
        h1, cls, ga, gb = _frames_mix(alpha, rb, a, u, wp, ps, h, wo, g1, b1, r1, x_outs, seq, n_sample)
        y = _moe(l, h1, cls.reshape(-1), ga.reshape(-1), gb.reshape(-1), w_gate, w_up, w_down)
        keep = state_pool.shape[2]
        last = (TM + (jnp.arange(bsz, dtype=I32)[:, None] + 1) * seq - keep + jnp.arange(keep, dtype=I32)[None, :])
        pools_p.append(_rows(u, last.reshape(-1)).reshape(bsz, keep, -1))
        pools_s.append(npool[:n_dec])
        ks_s.append(kx[:n_sample].reshape(n_dec, t_dec, n_heads, HEAD_W))
        vs_s.append(vx[:n_sample].reshape(n_dec, t_dec, n_heads, HEAD_W))

    y_frames, y_extra = _tail(alpha, h1, y, row(ln2_g[depth - 1]), row(ln2_b[depth - 1]))
    return (y_frames.reshape(bsz, seq, d), y_extra[:n_sample].reshape(n_dec, t_dec, d), k_out, v_out,
            jnp.stack(pools_p), jnp.stack(ks_s), jnp.stack(vs_s), jnp.stack(pools_s))
```
